```python
import jax
import jax.numpy as jnp
from jax import lax
import numpy as np

D_MODEL = 2048
BATCH = 1
SEQ = 8192
DEPTH = 1

GRID_W = 64
CTX_LEN = 256
N_MOD = 6
RMS_EPS = 1e-6
RWKV_HEAD = 64
RWKV_DIM = D_MODEL
RWKV_HEADS = RWKV_DIM // RWKV_HEAD
N_DIR = 2
DECAY_LORA = 96
ICLR_LORA = 96
GATE_LORA = 256
LNX_EPS = 64e-5
GMLP_DIM = D_MODEL // 2
CHUNK = 128
GMLP_GROUPS = 8
GMLP_GROUP_CH = GMLP_DIM // GMLP_GROUPS
LN_EPS = 1e-5
D_FF = 5632

OFF_K = RWKV_DIM
OFF_V = 2 * RWKV_DIM
OFF_DEC = 3 * RWKV_DIM
OFF_ICLR = OFF_DEC + N_DIR * DECAY_LORA
OFF_GATE = OFF_ICLR + N_DIR * ICLR_LORA
RWKV_COLS = OFF_GATE + GATE_LORA
OFF_GATES = RWKV_COLS + 2 * GMLP_DIM
IN_COLS = OFF_GATES + 2 * D_MODEL

kernel_name = "hybrid_rwkv7_chunkmlp_convffn_dit_layer"


def rmsnorm(x, g):
    xf = x.astype(jnp.float32)
    y = xf * lax.rsqrt(jnp.mean(xf * xf, axis=-1, keepdims=True) + RMS_EPS)
    return (y * g.astype(jnp.float32)).astype(x.dtype)


def dwconv1d(x, w):
    return lax.conv_general_dilated(x, w[:, None, :], window_strides=(1,), padding=((1, 1),),
                                    dimension_numbers=('NWC', 'WIO', 'NWC'),
                                    feature_group_count=x.shape[-1])


def dwconv2d_grid(x, w):
    B, T, C = x.shape
    rows = T // GRID_W
    y = lax.conv_general_dilated(x.reshape(B, rows, GRID_W, C), w[:, :, None, :],
                                 window_strides=(1, 1), padding=((1, 1), (1, 1)),
                                 dimension_numbers=('NHWC', 'HWIO', 'NHWC'),
                                 feature_group_count=C)
    return y.reshape(B, T, C)


def rwkv_terms(slab, w0, w_decay_up, a0, w_iclr_up, w_gate_up, k_k, k_a):
    f32 = jnp.float32
    B, T, _ = slab.shape
    r, k, v, dec, iclr, gate = jnp.split(slab, [OFF_K, OFF_V, OFF_DEC, OFF_ICLR, OFF_GATE], axis=-1)
    dec = dec.reshape(B, T, N_DIR, DECAY_LORA)
    iclr = iclr.reshape(B, T, N_DIR, ICLR_LORA)
    w_pre = (w0 + jnp.einsum('btdr,drc->btdc', jnp.tanh(dec), w_decay_up)).astype(f32)
    decay = jnp.exp(-jnp.exp(-jax.nn.softplus(-w_pre) - 0.5))
    a = jax.nn.sigmoid((a0 + jnp.einsum('btdr,drc->btdc', iclr, w_iclr_up)).astype(f32))
    g = jax.nn.sigmoid(gate) @ w_gate_up
    kk = (k * k_k).astype(f32).reshape(B, T, RWKV_HEADS, RWKV_HEAD)
    kk = kk * lax.rsqrt(jnp.maximum(jnp.sum(kk * kk, axis=-1, keepdims=True), 1e-24))
    k_dir = k.astype(f32)[:, :, None, :] * (1.0 + (a - 1.0) * k_a.astype(f32))
    heads = lambda z: z.reshape(*z.shape[:-1], RWKV_HEADS, RWKV_HEAD)
    a_h = heads(a)
    return (heads(r.astype(f32)), heads(k_dir), heads(v.astype(f32)), kk, heads(decay),
            kk[:, :, None] * a_h, g)


def wkv_scan(s0, r, w, k, v, kk, b, reverse):
    tm = lambda z: jnp.moveaxis(z, 1, 0)

    def step(S, inp):
        r_t, w_t, k_t, v_t, kk_t, b_t = inp
        sa = jnp.einsum('bhij,bhj->bhi', S, kk_t)
        S = S * w_t[:, :, None, :] - sa[..., None] * b_t[:, :, None, :] + v_t[..., None] * k_t[:, :, None, :]
        return S, jnp.einsum('bhij,bhj->bhi', S, r_t)

    s_final, ys = lax.scan(step, s0, (tm(r), tm(w), tm(k), tm(v), tm(kk), tm(b)), reverse=reverse)
    return s_final, jnp.moveaxis(ys, 0, 1)


def dir_inputs(terms, d):
    r, k_dir, v, kk, decay, b, _ = terms
    return r, decay[:, :, d], k_dir[:, :, d], v, kk, b[:, :, d]


def rwkv_readout(y, terms, r_k, lnx_g, lnx_b, dtype):
    r, k_dir, v, _, _, _, g = terms
    B, T = y.shape[:2]
    mu = jnp.mean(y, axis=-1, keepdims=True)
    var = jnp.mean(jnp.square(y - mu), axis=-1, keepdims=True)
    yn = ((y - mu) * lax.rsqrt(var + LNX_EPS)).reshape(B, T, RWKV_DIM) * lnx_g + lnx_b
    bonus = jnp.sum(r[:, :, None] * k_dir * r_k, axis=(2, 4))[..., None] * v
    return ((yn + bonus.reshape(B, T, RWKV_DIM)) * g).astype(dtype)


def rwkv_mixer(slab_x, slab_c, w0, w_decay_up, a0, w_iclr_up, w_gate_up, k_k, k_a, r_k,
               lnx_g, lnx_b, need_ctx):
    tx = rwkv_terms(slab_x, w0, w_decay_up, a0, w_iclr_up, w_gate_up, k_k, k_a)
    tc = rwkv_terms(slab_c, w0, w_decay_up, a0, w_iclr_up, w_gate_up, k_k, k_a)
    B = slab_x.shape[0]
    ys_x, ys_c = [], []
    for d in range(N_DIR):
        s0 = jnp.zeros((B, RWKV_HEADS, RWKV_HEAD, RWKV_HEAD), jnp.float32)
        s_ctx, yc = wkv_scan(s0, *dir_inputs(tc, d), reverse=(d == 1))
        _, yx = wkv_scan(s_ctx, *dir_inputs(tx, d), reverse=(d == 1))
        ys_x.append(yx)
        ys_c.append(yc)
    out_x = rwkv_readout(ys_x[0] + ys_x[1], tx, r_k, lnx_g, lnx_b, slab_x.dtype)
    out_c = rwkv_readout(ys_c[0] + ys_c[1], tc, r_k, lnx_g, lnx_b, slab_c.dtype) if need_ctx else None
    return out_x, out_c


def chunk_mlp(uv, ln_v_g, ln_v_b, w_spatial, b_spatial):
    z = jax.nn.gelu(uv, approximate=False)
    u, v = jnp.split(z, 2, axis=-1)
    B, T, _ = v.shape
    vf = v.astype(jnp.float32)
    mu = jnp.mean(vf, axis=-1, keepdims=True)
    var = jnp.mean(jnp.square(vf - mu), axis=-1, keepdims=True)
    v = ((vf - mu) * lax.rsqrt(var + LN_EPS) * ln_v_g + ln_v_b).astype(u.dtype)
    v = v.reshape(B, T // CHUNK, CHUNK, GMLP_GROUPS, GMLP_GROUP_CH)
    s = jnp.einsum('gpq,bnqgc->bnpgc', w_spatial, v) + b_spatial.T[None, None, :, :, None]
    return u * s.reshape(B, T, GMLP_DIM)


def merge_branches(p, y_rwkv, ln_v_g, ln_v_b, w_spatial, b_spatial, w_proj_a, w_proj_b, w_out):
    y_gmlp = chunk_mlp(p[..., RWKV_COLS:OFF_GATES], ln_v_g, ln_v_b, w_spatial, b_spatial)
    gate_a, gate_b = jnp.split(p[..., OFF_GATES:], 2, axis=-1)
    m = jax.nn.sigmoid(gate_a) * (y_rwkv @ w_proj_a) + jax.nn.sigmoid(gate_b) * (y_gmlp @ w_proj_b)
    return m @ w_out


def conv_glu_ffn(h, w_up, conv_w, w_down, grid):
    up = h @ w_up
    up = dwconv2d_grid(up, conv_w) if grid else dwconv1d(up, conv_w[1])
    a, b = jnp.split(up, 2, axis=-1)
    return (jax.nn.silu(a) * b) @ w_down


def trunk_layer(x, ctx, c, c_ctx, w_mod, b_mod, g_pre1, g_post1, g_pre2, g_post2, w_in, conv_rwkv,
                w0, w_decay_up, a0, w_iclr_up, w_gate_up, k_k, k_a, r_k, lnx_g, lnx_b,
                ln_v_g, ln_v_b, w_spatial, b_spatial, w_proj_a, w_proj_b, w_out,
                w_up, conv_ffn, w_down, need_ctx):
    mod = jax.nn.silu(c) @ w_mod + b_mod
    mod_c = jax.nn.silu(c_ctx) @ w_mod + b_mod
    sh1, sc1, gt1, sh2, sc2, gt2 = jnp.split(mod[:, None, :], N_MOD, axis=-1)
    csh1, csc1, cgt1, csh2, csc2, cgt2 = jnp.split(mod_c, N_MOD, axis=-1)

    h = rmsnorm(x, g_pre1) * (1.0 + sc1) + sh1
    hc = rmsnorm(ctx, g_pre1) * (1.0 + csc1) + csh1
    p = h @ w_in
    pc = hc @ (w_in if need_ctx else w_in[:, :RWKV_COLS])
    y_rwkv, yc_rwkv = rwkv_mixer(dwconv1d(p[..., :RWKV_COLS], conv_rwkv),
                                 dwconv1d(pc[..., :RWKV_COLS], conv_rwkv),
                                 w0, w_decay_up, a0, w_iclr_up, w_gate_up, k_k, k_a, r_k,
                                 lnx_g, lnx_b, need_ctx)
    mix = merge_branches(p, y_rwkv, ln_v_g, ln_v_b, w_spatial, b_spatial, w_proj_a, w_proj_b, w_out)
    x = x + gt1 * rmsnorm(mix, g_post1)

    h2 = rmsnorm(x, g_pre2) * (1.0 + sc2) + sh2
    x = x + gt2 * rmsnorm(conv_glu_ffn(h2, w_up, conv_ffn, w_down, grid=True), g_post2)

    if need_ctx:
        mix_c = merge_branches(pc, yc_rwkv, ln_v_g, ln_v_b, w_spatial, b_spatial, w_proj_a, w_proj_b, w_out)
        ctx = ctx + cgt1 * rmsnorm(mix_c, g_post1)
        h2c = rmsnorm(ctx, g_pre2) * (1.0 + csc2) + csh2
        ctx = ctx + cgt2 * rmsnorm(conv_glu_ffn(h2c, w_up, conv_ffn, w_down, grid=False), g_post2)
    return x, ctx


def setup_inputs(seed: int = 0) -> dict:
    key = jax.random.key(seed)
    ks = iter(jax.random.split(key, 40))
    L = DEPTH
    f32 = jnp.float32

    def nrm(shape, scale):
        return jax.random.normal(next(ks), shape, f32) * scale

    def gain(shape):
        return 1.0 + nrm(shape, 0.05)

    conv_base = jnp.array([0.3, 1.0, 0.3], f32)[None, :, None]
    ffn_base = jnp.zeros((3, 3), f32).at[1, 1].set(1.0)[None, :, :, None]
    return {
        "x": nrm((BATCH, SEQ, D_MODEL), 1.0),
        "c": nrm((BATCH, D_MODEL), 1.0),
        "ctx": nrm((BATCH, CTX_LEN, D_MODEL), 1.0),
        "c_ctx": nrm((D_MODEL,), 1.0),
        "w_mod": nrm((L, D_MODEL, N_MOD * D_MODEL), 0.5 * D_MODEL ** -0.5),
        "b_mod": nrm((L, N_MOD * D_MODEL), 0.01),
        "g_pre1": gain((L, D_MODEL)),
        "g_post1": gain((L, D_MODEL)),
        "g_pre2": gain((L, D_MODEL)),
        "g_post2": gain((L, D_MODEL)),
        "w_in": nrm((L, D_MODEL, IN_COLS), D_MODEL ** -0.5),
        "conv_rwkv": conv_base + nrm((L, 3, RWKV_COLS), 0.05),
        "w0": jax.random.uniform(next(ks), (L, N_DIR, RWKV_DIM), f32, -6.0, -1.0),
        "w_decay_up": nrm((L, N_DIR, DECAY_LORA, RWKV_DIM), 0.5 * DECAY_LORA ** -0.5),
        "a0": nrm((L, N_DIR, RWKV_DIM), 0.5),
        "w_iclr_up": nrm((L, N_DIR, ICLR_LORA, RWKV_DIM), 0.5 * ICLR_LORA ** -0.5),
        "w_gate_up": nrm((L, GATE_LORA, RWKV_DIM), GATE_LORA ** -0.5),
        "k_k": 0.85 + nrm((L, RWKV_DIM), 0.05),
        "k_a": gain((L, RWKV_DIM)),
        "r_k": nrm((L, RWKV_HEADS, RWKV_HEAD), 0.1),
        "lnx_g": gain((L, RWKV_DIM)),
        "lnx_b": nrm((L, RWKV_DIM), 0.01),
        "ln_v_g": gain((L, GMLP_DIM)),
        "ln_v_b": nrm((L, GMLP_DIM), 0.01),
        "w_spatial": nrm((L, GMLP_GROUPS, CHUNK, CHUNK), CHUNK ** -0.5),
        "b_spatial": gain((L, GMLP_GROUPS, CHUNK)),
        "w_proj_a": nrm((L, RWKV_DIM, D_MODEL), RWKV_DIM ** -0.5),
        "w_proj_b": nrm((L, GMLP_DIM, D_MODEL), GMLP_DIM ** -0.5),
        "w_out": nrm((L, D_MODEL, D_MODEL), D_MODEL ** -0.5),
        "w_up": nrm((L, D_MODEL, 2 * D_FF), D_MODEL ** -0.5),
        "conv_ffn": ffn_base + nrm((L, 3, 3, 2 * D_FF), 0.2),
        "w_down": nrm((L, D_FF, D_MODEL), D_FF ** -0.5),
    }


def reference(x, c, ctx, c_ctx, w_mod, b_mod, g_pre1, g_post1, g_pre2, g_post2, w_in, conv_rwkv,
              w0, w_decay_up, a0, w_iclr_up, w_gate_up, k_k, k_a, r_k, lnx_g, lnx_b,
              ln_v_g, ln_v_b, w_spatial, b_spatial, w_proj_a, w_proj_b, w_out,
              w_up, conv_ffn, w_down):
    for l in range(DEPTH):
        x, ctx = trunk_layer(x, ctx, c, c_ctx, w_mod[l], b_mod[l], g_pre1[l], g_post1[l], g_pre2[l],
                             g_post2[l], w_in[l], conv_rwkv[l], w0[l], w_decay_up[l], a0[l],
                             w_iclr_up[l], w_gate_up[l], k_k[l], k_a[l], r_k[l], lnx_g[l], lnx_b[l],
                             ln_v_g[l], ln_v_b[l], w_spatial[l], b_spatial[l], w_proj_a[l],
                             w_proj_b[l], w_out[l], w_up[l], conv_ffn[l], w_down[l],
                             need_ctx=(l < DEPTH - 1))
    return x
```

```python
import functools

import jax
import jax.numpy as jnp
from jax import lax
from jax.experimental import pallas as pl
from jax.experimental.pallas import tpu as pltpu

F32 = jnp.float32
BF16 = jnp.bfloat16

RMS_EPS = 1e-6
LNX_EPS = 64e-5
LN_EPS = 1e-5
HEAD = 64
GRID_W = 64
CHUNK = 128
GMLP_GROUPS = 8
DECAY_LORA = 96
ICLR_LORA = 96
GATE_LORA = 256
LORA_PAD = 128
LORA_COLS = 1024
EXP_M05 = 0.6065306597126334

SCAN_L = 64
SCAN_G = 4
SCAN_W = SCAN_G * HEAD

VMEM_LIMIT = 56 * 1024 * 1024


def _cparams(sem):
    return pltpu.CompilerParams(dimension_semantics=sem, vmem_limit_bytes=VMEM_LIMIT)


def _dot(a, b):
    return jnp.dot(a, b, preferred_element_type=F32)


def _dot_nt(a, b):
    return lax.dot_general(a, b, (((1,), (1,)), ((), ())), preferred_element_type=F32)


def _dot_tn(a, b):
    return lax.dot_general(a, b, (((0,), (0,)), ((), ())), preferred_element_type=F32)


def _sigmoid(x):
    return 1.0 / (1.0 + jnp.exp(-x))


def _split3(x):
    h = x.astype(BF16)
    r1 = x - h.astype(F32)
    m = r1.astype(BF16)
    l = (r1 - m.astype(F32)).astype(BF16)
    return h, m, l


def _segsum(x, ones_bd):
    rows, width = x.shape
    outs = []
    for g in range(width // SCAN_W):
        h, m, l = _split3(x[:, g * SCAN_W:(g + 1) * SCAN_W])
        s = _dot(jnp.concatenate([h, m, l], axis=0), ones_bd)
        outs.append(s[0:rows] + s[rows:2 * rows] + s[2 * rows:3 * rows])
    return outs[0] if len(outs) == 1 else jnp.concatenate(outs, axis=1)


def _mod_kernel(cc_ref, w_ref, b_ref, o_ref):
    s = cc_ref[...]
    s = s * _sigmoid(s)
    w = w_ref[...]
    r0 = jnp.sum(s[:, 0:1] * w, axis=0, keepdims=True)
    r1 = jnp.sum(s[:, 1:2] * w, axis=0, keepdims=True)
    o_ref[...] = jnp.concatenate([r0, r1], axis=0) + b_ref[...]


def _mod_call(cc, w_mod, b_mod):
    d, n = w_mod.shape
    tn = 1024
    return pl.pallas_call(
        _mod_kernel,
        grid=(n // tn,),
        in_specs=[pl.BlockSpec((d, 2), lambda j: (0, 0)),
                  pl.BlockSpec((d, tn), lambda j: (0, j)),
                  pl.BlockSpec((1, tn), lambda j: (0, j))],
        out_specs=pl.BlockSpec((2, tn), lambda j: (0, j)),
        out_shape=jax.ShapeDtypeStruct((2, n), F32),
        compiler_params=_cparams(("parallel",)),
        name="mod",
    )(cc, w_mod, b_mod)


def _modulated_rmsnorm(x, g, sc, sh):
    ms = jnp.mean(x * x, axis=-1, keepdims=True)
    return (x * lax.rsqrt(ms + RMS_EPS) * g) * (1.0 + sc) + sh


def _inproj_kernel(x_ref, g_ref, sc_ref, sh_ref, w_ref, o_ref, h_ref):
    @pl.when(pl.program_id(1) == 0)
    def _():
        h_ref[...] = _modulated_rmsnorm(x_ref[...], g_ref[...], sc_ref[...], sh_ref[...]).astype(BF16)

    o_ref[...] = _dot(h_ref[...], w_ref[...])


def _inproj_call(x, g, sc, sh, w, col_blocks, tm, tn):
    t, d = x.shape
    nj = len(col_blocks)
    if col_blocks == list(range(nj)):
        wmap = lambda i, j: (0, j)
    else:
        k, tail = nj - 1, col_blocks[-1]
        assert col_blocks[:k] == list(range(k))
        wmap = lambda i, j: (0, jnp.where(j < k, j, tail))
    vec = pl.BlockSpec((1, d), lambda i, j: (0, 0))
    return pl.pallas_call(
        _inproj_kernel,
        grid=(t // tm, nj),
        in_specs=[pl.BlockSpec((tm, d), lambda i, j: (i, 0)), vec, vec, vec,
                  pl.BlockSpec((d, tn), wmap)],
        out_specs=pl.BlockSpec((tm, tn), lambda i, j: (i, j)),
        out_shape=jax.ShapeDtypeStruct((t, nj * tn), F32),
        scratch_shapes=[pltpu.VMEM((tm, d), BF16)],
        compiler_params=_cparams(("parallel", "arbitrary")),
        name="inproj",
    )(x, g, sc, sh, w)


def _conv3(main, prev8, next8, w, first, last):
    tm = main.shape[0]
    rows = lax.broadcasted_iota(jnp.int32, main.shape, 0)
    before = jnp.where(first, 0.0, prev8[7:8, :])
    after = jnp.where(last, 0.0, next8[0:1, :])
    dn = jnp.where(rows == 0, before, pltpu.roll(main, 1, 0))
    up = jnp.where(rows == tm - 1, after, pltpu.roll(main, tm - 1, 0))
    return w[0:1, :] * dn + w[1:2, :] * main + w[2:3, :] * up


def _terms_kernel(pr, pr_p, pr_n, pk, pk_p, pk_n, pv, pv_p, pv_n, plo, plo_p, plo_n,
                  cw_r, cw_k, cw_v, cw_l, w0, wdec, a0, wiclr, wgate, k_k, k_a, r_k, ones_bd,
                  r_o, v_o, kk_o, lw_o, kd_o, b_o, g_o, bonus_o):
    i = pl.program_id(0)
    first = i == 0
    last = i == pl.num_programs(0) - 1
    r = _conv3(pr[...], pr_p[...], pr_n[...], cw_r[...], first, last)
    k = _conv3(pk[...], pk_p[...], pk_n[...], cw_k[...], first, last)
    v = _conv3(pv[...], pv_p[...], pv_n[...], cw_v[...], first, last)
    lora = _conv3(plo[...], plo_p[...], plo_n[...], cw_l[...], first, last)
    ones = ones_bd[...]
    r_o[...] = r
    v_o[...] = v
    kk = k * k_k[...]
    kk = kk * lax.rsqrt(jnp.maximum(_segsum(kk * kk, ones), 1e-24))
    kk_o[...] = kk
    gate = _sigmoid(lora[:, 4 * LORA_PAD:4 * LORA_PAD + GATE_LORA]).astype(BF16)
    g_o[...] = _dot(gate, wgate[...])
    ksum = None
    for d in range(2):
        dec = jnp.tanh(lora[:, d * LORA_PAD:(d + 1) * LORA_PAD]).astype(BF16)
        w_pre = w0[d:d + 1, :] + _dot(dec, wdec[d])
        lw_o[d] = -EXP_M05 * _sigmoid(w_pre)
        icl = lora[:, (2 + d) * LORA_PAD:(3 + d) * LORA_PAD].astype(BF16)
        a = _sigmoid(a0[d:d + 1, :] + _dot(icl, wiclr[d]))
        kd = k * (1.0 + (a - 1.0) * k_a[...])
        kd_o[d] = kd
        b_o[d] = kk * a
        ksum = kd if ksum is None else ksum + kd
    bonus_o[...] = _segsum(r * ksum * r_k[...], ones) * v


def _terms_call(p, lora_block, cw_r, cw_k, cw_v, cw_l, w0, wdec, a0, wiclr, wgate, k_k, k_a, r_k, ones_bd, tm):
    t = p.shape[0]
    d = k_k.shape[1]
    nb8 = t // 8
    r8 = tm // 8

    def trio(width, cb):
        return [pl.BlockSpec((tm, width), lambda i: (i, cb)),
                pl.BlockSpec((8, width), lambda i: (jnp.maximum(i * r8 - 1, 0), cb)),
                pl.BlockSpec((8, width), lambda i: (jnp.minimum((i + 1) * r8, nb8 - 1), cb))]

    def full(a):
        nd = a.ndim
        return pl.BlockSpec(a.shape, lambda i: (0,) * nd)

    consts = [cw_r, cw_k, cw_v, cw_l, w0, wdec, a0, wiclr, wgate, k_k, k_a, r_k, ones_bd]
    tok = pl.BlockSpec((tm, d), lambda i: (i, 0))
    tok2 = pl.BlockSpec((2, tm, d), lambda i: (0, i, 0))
    sd = jax.ShapeDtypeStruct((t, d), F32)
    sd2 = jax.ShapeDtypeStruct((2, t, d), F32)
    return pl.pallas_call(
        _terms_kernel,
        grid=(t // tm,),
        in_specs=trio(d, 0) + trio(d, 1) + trio(d, 2) + trio(LORA_COLS, lora_block) + [full(a) for a in consts],
        out_specs=[tok, tok, tok, tok2, tok2, tok2, tok, tok],
        out_shape=[sd, sd, sd, sd2, sd2, sd2, sd, sd],
        compiler_params=_cparams(("parallel",)),
        name="terms",
    )(*([p] * 12), *consts)


def _scan_kernel(r_ref, v_ref, kk_ref, lw_ref, kd_ref, b_ref, p0_ref, y_ref, pf_ref, state_ref,
                 *, reverse, nck):
    L, W, G = SCAN_L, SCAN_W, SCAN_G
    i = pl.program_id(1)

    @pl.when(i == 0)
    def _():
        state_ref[...] = p0_ref[...]

    row = lax.broadcasted_iota(jnp.int32, (W, W), 0)
    col = lax.broadcasted_iota(jnp.int32, (W, W), 1)
    bd = (row >> 6) == (col >> 6)
    eye_w = row == col
    tr = lax.broadcasted_iota(jnp.int32, (L, W), 0)
    tc = lax.broadcasted_iota(jnp.int32, (L, W), 1) & (L - 1)
    t1 = lax.broadcasted_iota(jnp.int32, (L, L), 0)
    t2 = lax.broadcasted_iota(jnp.int32, (L, L), 1)
    if reverse:
        strict, incl, tinc = tc > tr, tc >= tr, t2 >= t1
    else:
        strict, incl, tinc = tc < tr, tc <= tr, t2 <= t1
    tinc = jnp.where(tinc, 1.0, 0.0).astype(BF16)
    same16 = (tr >> 4) == (tc >> 4)
    same32 = (tr >> 5) == (tc >> 5)
    eye_row = jnp.where(tr == tc, 1.0, 0.0)

    def stack(x):
        return jnp.where(bd, jnp.concatenate([x] * G, axis=0), 0.0).astype(BF16)

    def mm(a, b_stacked):
        return _dot(a.astype(BF16), b_stacked)

    def prep(c):
        sl = pl.ds(c * L, L)
        r, v, kk = r_ref[sl, :], v_ref[sl, :], kk_ref[sl, :]
        lw, kd, b = lw_ref[sl, :], kd_ref[sl, :], b_ref[sl, :]
        h, m, l = _split3(lw)
        cs = _dot(tinc, h) + _dot(tinc, m) + _dot(tinc, l)
        tot = jnp.sum(lw, axis=0, keepdims=True)
        kap = kk * jnp.exp(cs - lw)
        rh = r * jnp.exp(cs)
        ginv = jnp.exp(-cs)
        gend = jnp.exp(tot - cs)
        s_bh = stack(b * ginv)
        s_kh = stack(kd * ginv)
        kap16, rh16 = kap.astype(BF16), rh.astype(BF16)
        akb = jnp.where(strict, _dot_nt(kap16, s_bh), 0.0)
        akk = jnp.where(strict, _dot_nt(kap16, s_kh), 0.0)
        ark = jnp.where(incl, _dot_nt(rh16, s_kh), 0.0)
        arb = jnp.where(incl, _dot_nt(rh16, s_bh), 0.0)
        dg = jnp.where(same16, akb, 0.0)
        d2 = mm(dg, stack(dg))
        d4 = mm(d2, stack(d2))
        tm_ = mm(eye_row - dg, stack(eye_row + d2))
        d8 = mm(d4, stack(d4))
        tm_ = mm(tm_, stack(eye_row + d4))
        tm_ = mm(tm_, stack(eye_row + d8))
        o1 = jnp.where(same16, 0.0, jnp.where(same32, akb, 0.0))
        tm_ = tm_ - mm(mm(tm_, stack(o1)), stack(tm_))
        o2 = jnp.where(same32, 0.0, akb)
        tm_ = tm_ - mm(mm(tm_, stack(o2)), stack(tm_))
        s_v = stack(v)
        akkv = mm(akk, s_v)
        kt = mm(tm_, stack(kap))
        u = mm(tm_, stack(akkv))
        rt = rh - mm(arb, stack(kt))
        yl = mm(ark, s_v) - mm(arb, stack(u))
        bt16 = (b * gend).astype(BF16)
        kt16 = (kd * gend).astype(BF16)
        m_raw = _dot_tn(bt16, kt.astype(BF16))
        c_raw = _dot_tn(kt16, v.astype(BF16)) - _dot_tn(bt16, u.astype(BF16))
        m_bd = jnp.where(eye_w, jnp.broadcast_to(jnp.exp(tot), (W, W)), 0.0) - jnp.where(bd, m_raw, 0.0)
        c_bd = jnp.where(bd, c_raw, 0.0)
        return rt.astype(BF16), yl, m_bd.astype(BF16), c_bd

    units = [prep(c) for c in range(nck)]
    p = state_ref[...]
    for c in (range(nck - 1, -1, -1) if reverse else range(nck)):
        rt, yl, m_bd, c_bd = units[c]
        p16 = p.astype(BF16)
        y_ref[pl.ds(c * L, L), :] = _dot(rt, p16) + yl
        p = _dot(m_bd, p16) + c_bd
    state_ref[...] = p
    pf_ref[...] = p


def _scan_call(r, v, kk, lw, kd, b, p0, d, tb):
    t, dm = r.shape
    ng = dm // SCAN_W
    nb = t // tb
    reverse = d == 1
    tok = (lambda g, i: (nb - 1 - i, g)) if reverse else (lambda g, i: (i, g))
    tok3 = (lambda g, i: (d, nb - 1 - i, g)) if reverse else (lambda g, i: (d, i, g))
    shared = pl.BlockSpec((tb, SCAN_W), tok)
    per_dir = pl.BlockSpec((None, tb, SCAN_W), tok3)
    st = pl.BlockSpec((None, SCAN_W, SCAN_W), lambda g, i: (g, 0, 0))
    return pl.pallas_call(
        functools.partial(_scan_kernel, reverse=reverse, nck=tb // SCAN_L),
        grid=(ng, nb),
        in_specs=[shared, shared, shared, per_dir, per_dir, per_dir, st],
        out_specs=[shared, st],
        out_shape=[jax.ShapeDtypeStruct((t, dm), F32), jax.ShapeDtypeStruct((ng, SCAN_W, SCAN_W), F32)],
        scratch_shapes=[pltpu.VMEM((SCAN_W, SCAN_W), F32)],
        compiler_params=_cparams(("parallel", "arbitrary")),
        name="scan_bwd" if reverse else "scan_fwd",
    )(r, v, kk, lw, kd, b, p0)


def _gelu(x):
    return 0.5 * x * (1.0 + lax.erf(x * 0.7071067811865476))


def _merge_kernel(y0, y1, bonus, g, uv, ga, gb, lnx_g, lnx_b, lnv_g, lnv_b, ws, bsp, wpa, wpb, ones_bd, m_o):
    ones = ones_bd[...]
    inv_n = 1.0 / HEAD
    y = y0[...] + y1[...]
    mu = _segsum(y, ones) * inv_n
    yc = y - mu
    var = _segsum(yc * yc, ones) * inv_n
    yn = yc * lax.rsqrt(var + LNX_EPS) * lnx_g[...] + lnx_b[...]
    y_rwkv = ((yn + bonus[...]) * g[...]).astype(BF16)
    branch_a = _dot(y_rwkv, wpa[...])

    z = _gelu(uv[...])
    half = z.shape[1] // 2
    u, vv = z[:, :half], z[:, half:]
    mu = jnp.mean(vv, axis=-1, keepdims=True)
    vc = vv - mu
    var = jnp.mean(vc * vc, axis=-1, keepdims=True)
    vn = (vc * lax.rsqrt(var + LN_EPS) * lnv_g[...] + lnv_b[...]).astype(BF16)
    gch = half // GMLP_GROUPS
    rows = []
    for ch in range(z.shape[0] // CHUNK):
        cols = [_dot(ws[gi], vn[ch * CHUNK:(ch + 1) * CHUNK, gi * gch:(gi + 1) * gch]) for gi in range(GMLP_GROUPS)]
        rows.append(jnp.concatenate(cols, axis=1) + bsp[...])
    s = rows[0] if len(rows) == 1 else jnp.concatenate(rows, axis=0)
    y_gmlp = (u * s).astype(BF16)
    branch_b = _dot(y_gmlp, wpb[...])
    m_o[...] = (_sigmoid(ga[...]) * branch_a + _sigmoid(gb[...]) * branch_b).astype(BF16)


def _merge_call(y0, y1, bonus, g, p, lnx_g, lnx_b, lnv_g, lnv_b, ws, bsp, wpa, wpb, ones_bd, tm):
    t, d = y0.shape

    def full(a):
        nd = a.ndim
        return pl.BlockSpec(a.shape, lambda i: (0,) * nd)

    tok = pl.BlockSpec((tm, d), lambda i: (i, 0))
    consts = [lnx_g, lnx_b, lnv_g, lnv_b, ws, bsp, wpa, wpb, ones_bd]
    return pl.pallas_call(
        _merge_kernel,
        grid=(t // tm,),
        in_specs=[tok, tok, tok, tok,
                  pl.BlockSpec((tm, d), lambda i: (i, 3)),
                  pl.BlockSpec((tm, d), lambda i: (i, 4)),
                  pl.BlockSpec((tm, d), lambda i: (i, 5)),
                  ] + [full(a) for a in consts],
        out_specs=tok,
        out_shape=jax.ShapeDtypeStruct((t, d), BF16),
        compiler_params=_cparams(("parallel",)),
        name="merge",
    )(y0, y1, bonus, g, p, p, p, *consts)


def _outproj_kernel(m, w, x, gt1, g_post1, g_pre2, sc2, sh2, x1_o, h2_o):
    mix = _dot(m[...], w[...])
    ms = jnp.mean(mix * mix, axis=-1, keepdims=True)
    x1 = x[...] + gt1[...] * (mix * lax.rsqrt(ms + RMS_EPS) * g_post1[...])
    x1_o[...] = x1
    h2_o[...] = _modulated_rmsnorm(x1, g_pre2[...], sc2[...], sh2[...]).astype(BF16)


def _outproj_call(m, w, x, gt1, g_post1, g_pre2, sc2, sh2, tm):
    t, d = x.shape
    tok = pl.BlockSpec((tm, d), lambda i: (i, 0))
    vec = pl.BlockSpec((1, d), lambda i: (0, 0))
    return pl.pallas_call(
        _outproj_kernel,
        grid=(t // tm,),
        in_specs=[tok, pl.BlockSpec((d, d), lambda i: (0, 0)), tok, vec, vec, vec, vec, vec],
        out_specs=[tok, tok],
        out_shape=[jax.ShapeDtypeStruct((t, d), F32), jax.ShapeDtypeStruct((t, d), BF16)],
        compiler_params=_cparams(("parallel",)),
        name="outproj",
    )(m, w, x, gt1, g_post1, g_pre2, sc2, sh2)


def _ffn_kernel(h_main, h_prev, h_next, wa, wb, cwa, cwb, wd, x1, gt2, g_post2, o_ref, hbuf, acc):
    i = pl.program_id(0)
    c = pl.program_id(1)
    tm = h_main.shape[0]
    rows_all = tm + 2 * GRID_W

    @pl.when(c == 0)
    def _():
        zero = jnp.zeros((GRID_W, h_main.shape[1]), BF16)
        hbuf[0:GRID_W, :] = jnp.where(i == 0, zero, h_prev[...])
        hbuf[GRID_W:GRID_W + tm, :] = h_main[...]
        hbuf[GRID_W + tm:rows_all, :] = jnp.where(i == pl.num_programs(0) - 1, zero, h_next[...])
        acc[...] = jnp.zeros_like(acc)

    h = hbuf[...]

    def conv_branch(w_ref, cw_ref):
        up = _dot(h, w_ref[...])
        gcol = lax.broadcasted_iota(jnp.int32, up.shape, 0) & (GRID_W - 1)
        left = jnp.where(gcol == 0, 0.0, pltpu.roll(up, 1, 0))
        right = jnp.where(gcol == GRID_W - 1, 0.0, pltpu.roll(up, rows_all - 1, 0))
        cw = cw_ref[...]
        out = None
        for dr in range(3):
            lo = dr * GRID_W
            term = (cw[3 * dr:3 * dr + 1, :] * left[lo:lo + tm, :]
                    + cw[3 * dr + 1:3 * dr + 2, :] * up[lo:lo + tm, :]
                    + cw[3 * dr + 2:3 * dr + 3, :] * right[lo:lo + tm, :])
            out = term if out is None else out + term
        return out

    a = conv_branch(wa, cwa)
    b = conv_branch(wb, cwb)
    glu = (a * _sigmoid(a) * b).astype(BF16)
    acc[...] += _dot(glu, wd[...])

    @pl.when(c == pl.num_programs(1) - 1)
    def _():
        f = acc[...]
        ms = jnp.mean(f * f, axis=-1, keepdims=True)
        o_ref[...] = x1[...] + gt2[...] * (f * lax.rsqrt(ms + RMS_EPS) * g_post2[...])


def _ffn_call(h2, w_up, cw, w_down, x1, gt2, g_post2, tm, fc):
    t, d = x1.shape
    dff = w_down.shape[0]
    nc = dff // fc
    hb = tm // GRID_W
    nhb = t // GRID_W
    tok = pl.BlockSpec((tm, d), lambda i, c: (i, 0))
    vec = pl.BlockSpec((1, d), lambda i, c: (0, 0))
    return pl.pallas_call(
        _ffn_kernel,
        grid=(t // tm, nc),
        in_specs=[tok,
                  pl.BlockSpec((GRID_W, d), lambda i, c: (jnp.maximum(i * hb - 1, 0), 0)),
                  pl.BlockSpec((GRID_W, d), lambda i, c: (jnp.minimum((i + 1) * hb, nhb - 1), 0)),
                  pl.BlockSpec((d, fc), lambda i, c: (0, c)),
                  pl.BlockSpec((d, fc), lambda i, c: (0, nc + c)),
                  pl.BlockSpec((9, fc), lambda i, c: (0, c)),
                  pl.BlockSpec((9, fc), lambda i, c: (0, nc + c)),
                  pl.BlockSpec((fc, d), lambda i, c: (c, 0)),
                  tok, vec, vec],
        out_specs=tok,
        out_shape=jax.ShapeDtypeStruct((t, d), F32),
        scratch_shapes=[pltpu.VMEM((tm + 2 * GRID_W, d), BF16), pltpu.VMEM((tm, d), F32)],
        compiler_params=_cparams(("parallel", "arbitrary")),
        name="ffn",
    )(h2, h2, h2, w_up, w_up, cw, cw, w_down, x1, gt2, g_post2)


def _pad_lora(a, axis):
    pad = [(0, 0)] * a.ndim
    pad[axis] = (0, LORA_PAD - a.shape[axis])
    return jnp.pad(a, pad)


def kernel(x, c, ctx, c_ctx, w_mod, b_mod, g_pre1, g_post1, g_pre2, g_post2, w_in, conv_rwkv, w0, w_decay_up, a0, w_iclr_up, w_gate_up, k_k, k_a, r_k, lnx_g, lnx_b, ln_v_g, ln_v_b, w_spatial, b_spatial, w_proj_a, w_proj_b, w_out, w_up, conv_ffn, w_down):
    assert x.shape[0] == 1 and w_mod.shape[0] == 1
    xt, ct = x[0], ctx[0]
    t, d = xt.shape
    tc = ct.shape[0]
    row = lambda a: a.reshape(1, -1)

    off_dec = 3 * d
    off_iclr = off_dec + 2 * DECAY_LORA
    off_gate = off_iclr + 2 * ICLR_LORA
    rwkv_cols = off_gate + GATE_LORA
    off_gates = rwkv_cols + d

    def lora_layout(a):
        parts = [_pad_lora(a[:, off_dec + k * DECAY_LORA:off_dec + (k + 1) * DECAY_LORA], 1) for k in range(2)]
        parts += [_pad_lora(a[:, off_iclr + k * ICLR_LORA:off_iclr + (k + 1) * ICLR_LORA], 1) for k in range(2)]
        parts += [a[:, off_gate:rwkv_cols], jnp.zeros((a.shape[0], LORA_COLS - 4 * LORA_PAD - GATE_LORA), a.dtype)]
        return jnp.concatenate(parts, axis=1)

    wi = w_in[0]
    w_in_p = jnp.concatenate([wi[:, :off_dec], wi[:, rwkv_cols:], lora_layout(wi)], axis=1).astype(BF16)
    cr = conv_rwkv[0]
    cw_r, cw_k, cw_v, cw_l = cr[:, :d], cr[:, d:2 * d], cr[:, 2 * d:3 * d], lora_layout(cr)
    wdec = _pad_lora(w_decay_up[0], 1).astype(BF16)
    wiclr = _pad_lora(w_iclr_up[0], 1).astype(BF16)
    wgate = w_gate_up[0].astype(BF16)
    ones_bd = jnp.kron(jnp.eye(SCAN_G, dtype=F32), jnp.ones((HEAD, HEAD), F32)).astype(BF16)
    gch = ln_v_g.shape[1] // GMLP_GROUPS
    bsp = jnp.repeat(b_spatial[0].T, gch, axis=1)
    cw_ffn = conv_ffn[0].reshape(9, -1)

    mod = _mod_call(jnp.stack([c[0], c_ctx], axis=1), w_mod[0], row(b_mod[0]))
    sh1, sc1, gt1, sh2, sc2, gt2 = [mod[0:1, k * d:(k + 1) * d] for k in range(6)]
    csh1, csc1 = mod[1:2, 0:d], mod[1:2, d:2 * d]

    n_blocks = w_in_p.shape[1] // 1024
    p = _inproj_call(xt, row(g_pre1[0]), sc1, sh1, w_in_p, list(range(n_blocks)), min(1024, t), 1024)
    pc = _inproj_call(ct, row(g_pre1[0]), csc1, csh1, w_in_p, list(range(6)) + [n_blocks - 1], min(256, tc), 1024)

    targs = (cw_r, cw_k, cw_v, cw_l, w0[0], wdec, a0[0], wiclr, wgate, row(k_k[0]), row(k_a[0]), row(r_k[0]), ones_bd)
    r_c, v_c, kk_c, lw_c, kd_c, b_c, _, _ = _terms_call(pc, 6, *targs, tm=min(128, tc))
    r_x, v_x, kk_x, lw_x, kd_x, b_x, g_x, bonus_x = _terms_call(p, n_blocks - 1, *targs, tm=128)
    zero_state = jnp.zeros((d // SCAN_W, SCAN_W, SCAN_W), F32)
    ys = []
    for dirn in range(2):
        _, s_ctx = _scan_call(r_c, v_c, kk_c, lw_c, kd_c, b_c, zero_state, dirn, min(256, tc))
        y_d, _ = _scan_call(r_x, v_x, kk_x, lw_x, kd_x, b_x, s_ctx, dirn, min(512, t))
        ys.append(y_d)

    m = _merge_call(ys[0], ys[1], bonus_x, g_x, p, row(lnx_g[0]), row(lnx_b[0]), row(ln_v_g[0]), row(ln_v_b[0]),
                    w_spatial[0].astype(BF16), bsp, w_proj_a[0].astype(BF16), w_proj_b[0].astype(BF16), ones_bd,
                    tm=128)
    x1, h2 = _outproj_call(m, w_out[0].astype(BF16), xt, gt1, row(g_post1[0]), row(g_pre2[0]), sc2, sh2,
                           tm=min(512, t))

    out = _ffn_call(h2, w_up[0].astype(BF16), cw_ffn, w_down[0].astype(BF16), x1, gt2, row(g_post2[0]),
                    tm=min(512, t), fc=512)
    return out[None]
```

```python
import functools

import jax
import jax.numpy as jnp
from jax import lax
from jax.experimental import pallas as pl
from jax.experimental.pallas import tpu as pltpu

F32 = jnp.float32
BF16 = jnp.bfloat16

RMS_EPS = 1e-6
LNX_EPS = 64e-5
LN_EPS = 1e-5
HEAD = 64
GRID_W = 64
CHUNK = 128
GMLP_GROUPS = 8
DECAY_LORA = 96
ICLR_LORA = 96
GATE_LORA = 256
LORA_PAD = 128
LORA_COLS = 1024
EXP_M05 = 0.6065306597126334

SCAN_L = 64
SCAN_G = 4
SCAN_W = SCAN_G * HEAD

VMEM_LIMIT = 56 * 1024 * 1024


def _cparams(sem):
    return pltpu.CompilerParams(dimension_semantics=sem, vmem_limit_bytes=VMEM_LIMIT)


def _dot(a, b):
    return jnp.dot(a, b, preferred_element_type=F32)


def _dot_nt(a, b):
    return lax.dot_general(a, b, (((1,), (1,)), ((), ())), preferred_element_type=F32)


def _dot_tn(a, b):
    return lax.dot_general(a, b, (((0,), (0,)), ((), ())), preferred_element_type=F32)


def _sigmoid(x):
    return 1.0 / (1.0 + jnp.exp(-x))


def _split3(x):
    h = x.astype(BF16)
    r1 = x - h.astype(F32)
    m = r1.astype(BF16)
    l = (r1 - m.astype(F32)).astype(BF16)
    return h, m, l


def _segsum(x, ones_bd):
    rows, width = x.shape
    outs = []
    for g in range(width // SCAN_W):
        h, m, l = _split3(x[:, g * SCAN_W:(g + 1) * SCAN_W])
        s = _dot(jnp.concatenate([h, m, l], axis=0), ones_bd)
        outs.append(s[0:rows] + s[rows:2 * rows] + s[2 * rows:3 * rows])
    return outs[0] if len(outs) == 1 else jnp.concatenate(outs, axis=1)


def _mod_kernel(cc_ref, w_ref, b_ref, o_ref):
    s = cc_ref[...]
    s = s * _sigmoid(s)
    w = w_ref[...]
    r0 = jnp.sum(s[:, 0:1] * w, axis=0, keepdims=True)
    r1 = jnp.sum(s[:, 1:2] * w, axis=0, keepdims=True)
    o_ref[...] = jnp.concatenate([r0, r1], axis=0) + b_ref[...]


def _mod_call(cc, w_mod, b_mod):
    d, n = w_mod.shape
    tn = 1024
    return pl.pallas_call(
        _mod_kernel,
        grid=(n // tn,),
        in_specs=[pl.BlockSpec((d, 2), lambda j: (0, 0)),
                  pl.BlockSpec((d, tn), lambda j: (0, j)),
                  pl.BlockSpec((1, tn), lambda j: (0, j))],
        out_specs=pl.BlockSpec((2, tn), lambda j: (0, j)),
        out_shape=jax.ShapeDtypeStruct((2, n), F32),
        compiler_params=_cparams(("parallel",)),
        name="mod",
    )(cc, w_mod, b_mod)


def _modulated_rmsnorm(x, g, sc, sh):
    ms = jnp.mean(x * x, axis=-1, keepdims=True)
    return (x * lax.rsqrt(ms + RMS_EPS) * g) * (1.0 + sc) + sh


def _inproj_kernel(x_ref, g_ref, sc_ref, sh_ref, w_ref, o_ref, h_ref):
    @pl.when(pl.program_id(1) == 0)
    def _():
        h_ref[...] = _modulated_rmsnorm(x_ref[...], g_ref[...], sc_ref[...], sh_ref[...]).astype(BF16)

    o_ref[...] = _dot(h_ref[...], w_ref[...])


def _inproj_call(x, g, sc, sh, w, col_blocks, tm, tn):
    t, d = x.shape
    nj = len(col_blocks)
    if col_blocks == list(range(nj)):
        wmap = lambda i, j: (0, j)
    else:
        k, tail = nj - 1, col_blocks[-1]
        assert col_blocks[:k] == list(range(k))
        wmap = lambda i, j: (0, jnp.where(j < k, j, tail))
    vec = pl.BlockSpec((1, d), lambda i, j: (0, 0))
    return pl.pallas_call(
        _inproj_kernel,
        grid=(t // tm, nj),
        in_specs=[pl.BlockSpec((tm, d), lambda i, j: (i, 0)), vec, vec, vec,
                  pl.BlockSpec((d, tn), wmap)],
        out_specs=pl.BlockSpec((tm, tn), lambda i, j: (i, j)),
        out_shape=jax.ShapeDtypeStruct((t, nj * tn), F32),
        scratch_shapes=[pltpu.VMEM((tm, d), BF16)],
        compiler_params=_cparams(("parallel", "arbitrary")),
        name="inproj",
    )(x, g, sc, sh, w)


def _conv3(main, prev8, next8, w, first, last):
    tm = main.shape[0]
    rows = lax.broadcasted_iota(jnp.int32, main.shape, 0)
    before = jnp.where(first, 0.0, prev8[7:8, :])
    after = jnp.where(last, 0.0, next8[0:1, :])
    dn = jnp.where(rows == 0, before, pltpu.roll(main, 1, 0))
    up = jnp.where(rows == tm - 1, after, pltpu.roll(main, tm - 1, 0))
    return w[0:1, :] * dn + w[1:2, :] * main + w[2:3, :] * up


def _terms_kernel(pr, pr_p, pr_n, pk, pk_p, pk_n, pv, pv_p, pv_n, plo, plo_p, plo_n,
                  cw_r, cw_k, cw_v, cw_l, w0, wdec, a0, wiclr, wgate, k_k, k_a, r_k, ones_bd,
                  r_o, v_o, kk_o, lw_o, kd_o, b_o, g_o, bonus_o):
    i = pl.program_id(0)
    first = i == 0
    last = i == pl.num_programs(0) - 1
    r = _conv3(pr[...], pr_p[...], pr_n[...], cw_r[...], first, last)
    k = _conv3(pk[...], pk_p[...], pk_n[...], cw_k[...], first, last)
    v = _conv3(pv[...], pv_p[...], pv_n[...], cw_v[...], first, last)
    lora = _conv3(plo[...], plo_p[...], plo_n[...], cw_l[...], first, last)
    ones = ones_bd[...]
    r_o[...] = r
    v_o[...] = v
    kk = k * k_k[...]
    kk = kk * lax.rsqrt(jnp.maximum(_segsum(kk * kk, ones), 1e-24))
    kk_o[...] = kk
    gate = _sigmoid(lora[:, 4 * LORA_PAD:4 * LORA_PAD + GATE_LORA]).astype(BF16)
    g_o[...] = _dot(gate, wgate[...])
    ksum = None
    for d in range(2):
        dec = jnp.tanh(lora[:, d * LORA_PAD:(d + 1) * LORA_PAD]).astype(BF16)
        w_pre = w0[d:d + 1, :] + _dot(dec, wdec[d])
        lw_o[d] = -EXP_M05 * _sigmoid(w_pre)
        icl = lora[:, (2 + d) * LORA_PAD:(3 + d) * LORA_PAD].astype(BF16)
        a = _sigmoid(a0[d:d + 1, :] + _dot(icl, wiclr[d]))
        kd = k * (1.0 + (a - 1.0) * k_a[...])
        kd_o[d] = kd
        b_o[d] = kk * a
        ksum = kd if ksum is None else ksum + kd
    bonus_o[...] = _segsum(r * ksum * r_k[...], ones) * v


def _terms_call(p, lora_block, cw_r, cw_k, cw_v, cw_l, w0, wdec, a0, wiclr, wgate, k_k, k_a, r_k, ones_bd, tm):
    t = p.shape[0]
    d = k_k.shape[1]
    nb8 = t // 8
    r8 = tm // 8

    def trio(width, cb):
        return [pl.BlockSpec((tm, width), lambda i: (i, cb)),
                pl.BlockSpec((8, width), lambda i: (jnp.maximum(i * r8 - 1, 0), cb)),
                pl.BlockSpec((8, width), lambda i: (jnp.minimum((i + 1) * r8, nb8 - 1), cb))]

    def full(a):
        nd = a.ndim
        return pl.BlockSpec(a.shape, lambda i: (0,) * nd)

    consts = [cw_r, cw_k, cw_v, cw_l, w0, wdec, a0, wiclr, wgate, k_k, k_a, r_k, ones_bd]
    tok = pl.BlockSpec((tm, d), lambda i: (i, 0))
    tok2 = pl.BlockSpec((2, tm, d), lambda i: (0, i, 0))
    sd = jax.ShapeDtypeStruct((t, d), F32)
    sd2 = jax.ShapeDtypeStruct((2, t, d), F32)
    return pl.pallas_call(
        _terms_kernel,
        grid=(t // tm,),
        in_specs=trio(d, 0) + trio(d, 1) + trio(d, 2) + trio(LORA_COLS, lora_block) + [full(a) for a in consts],
        out_specs=[tok, tok, tok, tok2, tok2, tok2, tok, tok],
        out_shape=[sd, sd, sd, sd2, sd2, sd2, sd, sd],
        compiler_params=_cparams(("parallel",)),
        name="terms",
    )(*([p] * 12), *consts)


def _scan_kernel(r_ref, v_ref, kk_ref, lw_ref, kd_ref, b_ref, p0_ref, y_ref, pf_ref,
                 state_ref, rt_s, yl_s, m_s, c_s, *, reverse, nck):
    L, W, G = SCAN_L, SCAN_W, SCAN_G
    i = pl.program_id(1)
    nb = pl.num_programs(1) - 1
    slot_w = i % 2
    slot_r = 1 - slot_w

    @pl.when(i == 0)
    def _():
        state_ref[...] = p0_ref[...]
        rt_s[1] = jnp.zeros(rt_s.shape[1:], rt_s.dtype)
        yl_s[1] = jnp.zeros(yl_s.shape[1:], yl_s.dtype)
        m_s[1] = jnp.zeros(m_s.shape[1:], m_s.dtype)
        c_s[1] = jnp.zeros(c_s.shape[1:], c_s.dtype)

    row = lax.broadcasted_iota(jnp.int32, (W, W), 0)
    col = lax.broadcasted_iota(jnp.int32, (W, W), 1)
    bd = (row >> 6) == (col >> 6)
    eye_w = row == col
    tr = lax.broadcasted_iota(jnp.int32, (L, W), 0)
    tc = lax.broadcasted_iota(jnp.int32, (L, W), 1) & (L - 1)
    t1 = lax.broadcasted_iota(jnp.int32, (L, L), 0)
    t2 = lax.broadcasted_iota(jnp.int32, (L, L), 1)
    if reverse:
        strict, incl, tinc = tc > tr, tc >= tr, t2 >= t1
    else:
        strict, incl, tinc = tc < tr, tc <= tr, t2 <= t1
    tinc = jnp.where(tinc, 1.0, 0.0).astype(BF16)
    same16 = (tr >> 4) == (tc >> 4)
    same32 = (tr >> 5) == (tc >> 5)
    eye_row = jnp.where(tr == tc, 1.0, 0.0)

    def stack(x):
        return jnp.where(bd, jnp.concatenate([x] * G, axis=0), 0.0).astype(BF16)

    def mm(a, b_stacked):
        return _dot(a.astype(BF16), b_stacked)

    def prep(c):
        sl = pl.ds(c * L, L)
        lw = lw_ref[sl, :]
        h, m, l = _split3(lw)
        cs = _dot(tinc, h) + _dot(tinc, m) + _dot(tinc, l)
        yield
        r, v, kk = r_ref[sl, :], v_ref[sl, :], kk_ref[sl, :]
        kd, b = kd_ref[sl, :], b_ref[sl, :]
        tot = jnp.sum(lw, axis=0, keepdims=True)
        kap = kk * jnp.exp(cs - lw)
        rh = r * jnp.exp(cs)
        ginv = jnp.exp(-cs)
        gend = jnp.exp(tot - cs)
        s_bh = stack(b * ginv)
        s_kh = stack(kd * ginv)
        kr16 = jnp.concatenate([kap, rh], axis=0).astype(BF16)
        a_b = _dot_nt(kr16, s_bh)
        a_k = _dot_nt(kr16, s_kh)
        yield
        akb = jnp.where(strict, a_b[0:L], 0.0)
        arb = jnp.where(incl, a_b[L:2 * L], 0.0)
        akk = jnp.where(strict, a_k[0:L], 0.0)
        ark = jnp.where(incl, a_k[L:2 * L], 0.0)
        dg = jnp.where(same16, akb, 0.0)
        s_v = stack(v)
        d2 = mm(dg, stack(dg))
        akkv = mm(akk, s_v)
        yield
        d4 = mm(d2, stack(d2))
        tm_ = mm(eye_row - dg, stack(eye_row + d2))
        yield
        d8 = mm(d4, stack(d4))
        tm_ = mm(tm_, stack(eye_row + d4))
        yield
        tm_ = mm(tm_, stack(eye_row + d8))
        yield
        o1 = jnp.where(same16, 0.0, jnp.where(same32, akb, 0.0))
        x1 = mm(tm_, stack(o1))
        yield
        tm_ = tm_ - mm(x1, stack(tm_))
        yield
        o2 = jnp.where(same32, 0.0, akb)
        x2 = mm(tm_, stack(o2))
        yield
        tm_ = tm_ - mm(x2, stack(tm_))
        yield
        kt = mm(tm_, stack(kap))
        u = mm(tm_, stack(akkv))
        yield
        rt = rh - mm(arb, stack(kt))
        yl = mm(ark, s_v) - mm(arb, stack(u))
        bt16 = (b * gend).astype(BF16)
        kb16 = jnp.concatenate([(kd * gend).astype(BF16), bt16], axis=0)
        vu16 = jnp.concatenate([v, -u], axis=0).astype(BF16)
        m_raw = _dot_tn(bt16, kt.astype(BF16))
        c_raw = _dot_tn(kb16, vu16)
        yield
        m_bd = jnp.where(eye_w, jnp.broadcast_to(jnp.exp(tot), (W, W)), 0.0) - jnp.where(bd, m_raw, 0.0)
        rt_s[slot_w, c] = rt.astype(BF16)
        yl_s[slot_w, c] = yl
        m_s[slot_w, c] = m_bd.astype(BF16)
        c_s[slot_w, c] = jnp.where(bd, c_raw, 0.0)

    def state_pass():
        p = state_ref[...]
        for c in (range(nck - 1, -1, -1) if reverse else range(nck)):
            p16 = p.astype(BF16)
            y_ref[pl.ds(c * L, L), :] = _dot(rt_s[slot_r, c], p16) + yl_s[slot_r, c]
            p = jnp.where(i > 0, _dot(m_s[slot_r, c], p16) + c_s[slot_r, c], p)
            yield
        state_ref[...] = p
        pf_ref[...] = p

    def run(gens):
        while gens:
            still = []
            for gen in gens:
                try:
                    next(gen)
                    still.append(gen)
                except StopIteration:
                    pass
            gens = still

    @pl.when(i < nb)
    def _():
        run([state_pass()] + [prep(c) for c in range(nck)])

    @pl.when(i == nb)
    def _():
        run([state_pass()])


def _scan_call(r, v, kk, lw, kd, b, p0, d, tb):
    t, dm = r.shape
    ng = dm // SCAN_W
    nb = t // tb
    reverse = d == 1
    nck = tb // SCAN_L

    def block(step):
        step = jnp.clip(step, 0, nb - 1)
        return nb - 1 - step if reverse else step

    shared = pl.BlockSpec((tb, SCAN_W), lambda g, i: (block(i), g))
    per_dir = pl.BlockSpec((None, tb, SCAN_W), lambda g, i: (d, block(i), g))
    lagged = pl.BlockSpec((tb, SCAN_W), lambda g, i: (block(i - 1), g))
    st = pl.BlockSpec((None, SCAN_W, SCAN_W), lambda g, i: (g, 0, 0))
    return pl.pallas_call(
        functools.partial(_scan_kernel, reverse=reverse, nck=nck),
        grid=(ng, nb + 1),
        in_specs=[shared, shared, shared, per_dir, per_dir, per_dir, st],
        out_specs=[lagged, st],
        out_shape=[jax.ShapeDtypeStruct((t, dm), F32), jax.ShapeDtypeStruct((ng, SCAN_W, SCAN_W), F32)],
        scratch_shapes=[pltpu.VMEM((SCAN_W, SCAN_W), F32),
                        pltpu.VMEM((2, nck, SCAN_L, SCAN_W), BF16),
                        pltpu.VMEM((2, nck, SCAN_L, SCAN_W), F32),
                        pltpu.VMEM((2, nck, SCAN_W, SCAN_W), BF16),
                        pltpu.VMEM((2, nck, SCAN_W, SCAN_W), F32)],
        compiler_params=_cparams(("parallel", "arbitrary")),
        name="scan_bwd" if reverse else "scan_fwd",
    )(r, v, kk, lw, kd, b, p0)


def _gelu(x):
    return 0.5 * x * (1.0 + lax.erf(x * 0.7071067811865476))


def _merge_kernel(y0, y1, bonus, g, uv, ga, gb, lnx_g, lnx_b, lnv_g, lnv_b, ws, bsp, wpa, wpb, ones_bd, m_o):
    ones = ones_bd[...]
    inv_n = 1.0 / HEAD
    y = y0[...] + y1[...]
    mu = _segsum(y, ones) * inv_n
    yc = y - mu
    var = _segsum(yc * yc, ones) * inv_n
    yn = yc * lax.rsqrt(var + LNX_EPS) * lnx_g[...] + lnx_b[...]
    y_rwkv = ((yn + bonus[...]) * g[...]).astype(BF16)
    branch_a = _dot(y_rwkv, wpa[...])

    z = _gelu(uv[...])
    half = z.shape[1] // 2
    u, vv = z[:, :half], z[:, half:]
    mu = jnp.mean(vv, axis=-1, keepdims=True)
    vc = vv - mu
    var = jnp.mean(vc * vc, axis=-1, keepdims=True)
    vn = (vc * lax.rsqrt(var + LN_EPS) * lnv_g[...] + lnv_b[...]).astype(BF16)
    gch = half // GMLP_GROUPS
    rows = []
    for ch in range(z.shape[0] // CHUNK):
        cols = [_dot(ws[gi], vn[ch * CHUNK:(ch + 1) * CHUNK, gi * gch:(gi + 1) * gch]) for gi in range(GMLP_GROUPS)]
        rows.append(jnp.concatenate(cols, axis=1) + bsp[...])
    s = rows[0] if len(rows) == 1 else jnp.concatenate(rows, axis=0)
    y_gmlp = (u * s).astype(BF16)
    branch_b = _dot(y_gmlp, wpb[...])
    m_o[...] = (_sigmoid(ga[...]) * branch_a + _sigmoid(gb[...]) * branch_b).astype(BF16)


def _merge_call(y0, y1, bonus, g, p, lnx_g, lnx_b, lnv_g, lnv_b, ws, bsp, wpa, wpb, ones_bd, tm):
    t, d = y0.shape

    def full(a):
        nd = a.ndim
        return pl.BlockSpec(a.shape, lambda i: (0,) * nd)

    tok = pl.BlockSpec((tm, d), lambda i: (i, 0))
    consts = [lnx_g, lnx_b, lnv_g, lnv_b, ws, bsp, wpa, wpb, ones_bd]
    return pl.pallas_call(
        _merge_kernel,
        grid=(t // tm,),
        in_specs=[tok, tok, tok, tok,
                  pl.BlockSpec((tm, d), lambda i: (i, 3)),
                  pl.BlockSpec((tm, d), lambda i: (i, 4)),
                  pl.BlockSpec((tm, d), lambda i: (i, 5)),
                  ] + [full(a) for a in consts],
        out_specs=tok,
        out_shape=jax.ShapeDtypeStruct((t, d), BF16),
        compiler_params=_cparams(("parallel",)),
        name="merge",
    )(y0, y1, bonus, g, p, p, p, *consts)


def _outproj_kernel(m, w, x, gt1, g_post1, g_pre2, sc2, sh2, x1_o, h2_o):
    mix = _dot(m[...], w[...])
    ms = jnp.mean(mix * mix, axis=-1, keepdims=True)
    x1 = x[...] + gt1[...] * (mix * lax.rsqrt(ms + RMS_EPS) * g_post1[...])
    x1_o[...] = x1
    h2_o[...] = _modulated_rmsnorm(x1, g_pre2[...], sc2[...], sh2[...]).astype(BF16)


def _outproj_call(m, w, x, gt1, g_post1, g_pre2, sc2, sh2, tm):
    t, d = x.shape
    tok = pl.BlockSpec((tm, d), lambda i: (i, 0))
    vec = pl.BlockSpec((1, d), lambda i: (0, 0))
    return pl.pallas_call(
        _outproj_kernel,
        grid=(t // tm,),
        in_specs=[tok, pl.BlockSpec((d, d), lambda i: (0, 0)), tok, vec, vec, vec, vec, vec],
        out_specs=[tok, tok],
        out_shape=[jax.ShapeDtypeStruct((t, d), F32), jax.ShapeDtypeStruct((t, d), BF16)],
        compiler_params=_cparams(("parallel",)),
        name="outproj",
    )(m, w, x, gt1, g_post1, g_pre2, sc2, sh2)


def _ffn_kernel(h_main, h_prev, h_next, wa, wb, cwa, cwb, wd, x1, gt2, g_post2, o_ref, hbuf, acc):
    i = pl.program_id(0)
    c = pl.program_id(1)
    tm = h_main.shape[0]
    rows_all = tm + 2 * GRID_W

    @pl.when(c == 0)
    def _():
        zero = jnp.zeros((GRID_W, h_main.shape[1]), BF16)
        hbuf[0:GRID_W, :] = jnp.where(i == 0, zero, h_prev[...])
        hbuf[GRID_W:GRID_W + tm, :] = h_main[...]
        hbuf[GRID_W + tm:rows_all, :] = jnp.where(i == pl.num_programs(0) - 1, zero, h_next[...])
        acc[...] = jnp.zeros_like(acc)

    h = hbuf[...]

    def conv_branch(w_ref, cw_ref):
        up = _dot(h, w_ref[...])
        gcol = lax.broadcasted_iota(jnp.int32, up.shape, 0) & (GRID_W - 1)
        left = jnp.where(gcol == 0, 0.0, pltpu.roll(up, 1, 0))
        right = jnp.where(gcol == GRID_W - 1, 0.0, pltpu.roll(up, rows_all - 1, 0))
        cw = cw_ref[...]
        out = None
        for dr in range(3):
            lo = dr * GRID_W
            term = (cw[3 * dr:3 * dr + 1, :] * left[lo:lo + tm, :]
                    + cw[3 * dr + 1:3 * dr + 2, :] * up[lo:lo + tm, :]
                    + cw[3 * dr + 2:3 * dr + 3, :] * right[lo:lo + tm, :])
            out = term if out is None else out + term
        return out

    a = conv_branch(wa, cwa)
    b = conv_branch(wb, cwb)
    glu = (a * _sigmoid(a) * b).astype(BF16)
    acc[...] += _dot(glu, wd[...])

    @pl.when(c == pl.num_programs(1) - 1)
    def _():
        f = acc[...]
        ms = jnp.mean(f * f, axis=-1, keepdims=True)
        o_ref[...] = x1[...] + gt2[...] * (f * lax.rsqrt(ms + RMS_EPS) * g_post2[...])


def _ffn_call(h2, w_up, cw, w_down, x1, gt2, g_post2, tm, fc):
    t, d = x1.shape
    dff = w_down.shape[0]
    nc = dff // fc
    hb = tm // GRID_W
    nhb = t // GRID_W
    tok = pl.BlockSpec((tm, d), lambda i, c: (i, 0))
    vec = pl.BlockSpec((1, d), lambda i, c: (0, 0))
    return pl.pallas_call(
        _ffn_kernel,
        grid=(t // tm, nc),
        in_specs=[tok,
                  pl.BlockSpec((GRID_W, d), lambda i, c: (jnp.maximum(i * hb - 1, 0), 0)),
                  pl.BlockSpec((GRID_W, d), lambda i, c: (jnp.minimum((i + 1) * hb, nhb - 1), 0)),
                  pl.BlockSpec((d, fc), lambda i, c: (0, c)),
                  pl.BlockSpec((d, fc), lambda i, c: (0, nc + c)),
                  pl.BlockSpec((9, fc), lambda i, c: (0, c)),
                  pl.BlockSpec((9, fc), lambda i, c: (0, nc + c)),
                  pl.BlockSpec((fc, d), lambda i, c: (c, 0)),
                  tok, vec, vec],
        out_specs=tok,
        out_shape=jax.ShapeDtypeStruct((t, d), F32),
        scratch_shapes=[pltpu.VMEM((tm + 2 * GRID_W, d), BF16), pltpu.VMEM((tm, d), F32)],
        compiler_params=_cparams(("parallel", "arbitrary")),
        name="ffn",
    )(h2, h2, h2, w_up, w_up, cw, cw, w_down, x1, gt2, g_post2)


def _pad_lora(a, axis):
    pad = [(0, 0)] * a.ndim
    pad[axis] = (0, LORA_PAD - a.shape[axis])
    return jnp.pad(a, pad)


def kernel(x, c, ctx, c_ctx, w_mod, b_mod, g_pre1, g_post1, g_pre2, g_post2, w_in, conv_rwkv, w0, w_decay_up, a0, w_iclr_up, w_gate_up, k_k, k_a, r_k, lnx_g, lnx_b, ln_v_g, ln_v_b, w_spatial, b_spatial, w_proj_a, w_proj_b, w_out, w_up, conv_ffn, w_down):
    assert x.shape[0] == 1 and w_mod.shape[0] == 1
    xt, ct = x[0], ctx[0]
    t, d = xt.shape
    tc = ct.shape[0]
    row = lambda a: a.reshape(1, -1)

    off_dec = 3 * d
    off_iclr = off_dec + 2 * DECAY_LORA
    off_gate = off_iclr + 2 * ICLR_LORA
    rwkv_cols = off_gate + GATE_LORA
    off_gates = rwkv_cols + d

    def lora_layout(a):
        parts = [_pad_lora(a[:, off_dec + k * DECAY_LORA:off_dec + (k + 1) * DECAY_LORA], 1) for k in range(2)]
        parts += [_pad_lora(a[:, off_iclr + k * ICLR_LORA:off_iclr + (k + 1) * ICLR_LORA], 1) for k in range(2)]
        parts += [a[:, off_gate:rwkv_cols], jnp.zeros((a.shape[0], LORA_COLS - 4 * LORA_PAD - GATE_LORA), a.dtype)]
        return jnp.concatenate(parts, axis=1)

    wi = w_in[0]
    w_in_p = jnp.concatenate([wi[:, :off_dec], wi[:, rwkv_cols:], lora_layout(wi)], axis=1).astype(BF16)
    cr = conv_rwkv[0]
    cw_r, cw_k, cw_v, cw_l = cr[:, :d], cr[:, d:2 * d], cr[:, 2 * d:3 * d], lora_layout(cr)
    wdec = _pad_lora(w_decay_up[0], 1).astype(BF16)
    wiclr = _pad_lora(w_iclr_up[0], 1).astype(BF16)
    wgate = w_gate_up[0].astype(BF16)
    ones_bd = jnp.kron(jnp.eye(SCAN_G, dtype=F32), jnp.ones((HEAD, HEAD), F32)).astype(BF16)
    gch = ln_v_g.shape[1] // GMLP_GROUPS
    bsp = jnp.repeat(b_spatial[0].T, gch, axis=1)
    cw_ffn = conv_ffn[0].reshape(9, -1)

    mod = _mod_call(jnp.stack([c[0], c_ctx], axis=1), w_mod[0], row(b_mod[0]))
    sh1, sc1, gt1, sh2, sc2, gt2 = [mod[0:1, k * d:(k + 1) * d] for k in range(6)]
    csh1, csc1 = mod[1:2, 0:d], mod[1:2, d:2 * d]

    n_blocks = w_in_p.shape[1] // 1024
    p = _inproj_call(xt, row(g_pre1[0]), sc1, sh1, w_in_p, list(range(n_blocks)), min(1024, t), 1024)
    pc = _inproj_call(ct, row(g_pre1[0]), csc1, csh1, w_in_p, list(range(6)) + [n_blocks - 1], min(256, tc), 1024)

    targs = (cw_r, cw_k, cw_v, cw_l, w0[0], wdec, a0[0], wiclr, wgate, row(k_k[0]), row(k_a[0]), row(r_k[0]), ones_bd)
    r_c, v_c, kk_c, lw_c, kd_c, b_c, _, _ = _terms_call(pc, 6, *targs, tm=min(128, tc))
    r_x, v_x, kk_x, lw_x, kd_x, b_x, g_x, bonus_x = _terms_call(p, n_blocks - 1, *targs, tm=128)
    zero_state = jnp.zeros((d // SCAN_W, SCAN_W, SCAN_W), F32)
    ys = []
    for dirn in range(2):
        _, s_ctx = _scan_call(r_c, v_c, kk_c, lw_c, kd_c, b_c, zero_state, dirn, min(256, tc))
        y_d, _ = _scan_call(r_x, v_x, kk_x, lw_x, kd_x, b_x, s_ctx, dirn, min(512, t))
        ys.append(y_d)

    m = _merge_call(ys[0], ys[1], bonus_x, g_x, p, row(lnx_g[0]), row(lnx_b[0]), row(ln_v_g[0]), row(ln_v_b[0]),
                    w_spatial[0].astype(BF16), bsp, w_proj_a[0].astype(BF16), w_proj_b[0].astype(BF16), ones_bd,
                    tm=128)
    x1, h2 = _outproj_call(m, w_out[0].astype(BF16), xt, gt1, row(g_post1[0]), row(g_pre2[0]), sc2, sh2,
                           tm=min(512, t))

    out = _ffn_call(h2, w_up[0].astype(BF16), cw_ffn, w_down[0].astype(BF16), x1, gt2, row(g_post2[0]),
                    tm=min(512, t), fc=512)
    return out[None]
```

```python
import functools

import jax
import jax.numpy as jnp
from jax import lax
from jax.experimental import pallas as pl
from jax.experimental.pallas import tpu as pltpu

F32 = jnp.float32
BF16 = jnp.bfloat16

RMS_EPS = 1e-6
LNX_EPS = 64e-5
LN_EPS = 1e-5
HEAD = 64
GRID_W = 64
CHUNK = 128
GMLP_GROUPS = 8
DECAY_LORA = 96
ICLR_LORA = 96
GATE_LORA = 256
LORA_PAD = 128
LORA_COLS = 1024
EXP_M05 = 0.6065306597126334

SCAN_L = 64
SCAN_G = 4
SCAN_W = SCAN_G * HEAD

VMEM_LIMIT = 56 * 1024 * 1024


def _cparams(sem):
    return pltpu.CompilerParams(dimension_semantics=sem, vmem_limit_bytes=VMEM_LIMIT)


def _dot(a, b):
    return jnp.dot(a, b, preferred_element_type=F32)


def _dot_nt(a, b):
    return lax.dot_general(a, b, (((1,), (1,)), ((), ())), preferred_element_type=F32)


def _dot_tn(a, b):
    return lax.dot_general(a, b, (((0,), (0,)), ((), ())), preferred_element_type=F32)


def _sigmoid(x):
    return 1.0 / (1.0 + jnp.exp(-x))


def _split3(x):
    h = x.astype(BF16)
    r1 = x - h.astype(F32)
    m = r1.astype(BF16)
    l = (r1 - m.astype(F32)).astype(BF16)
    return h, m, l


def _segsum(x, ones_bd):
    rows, width = x.shape
    outs = []
    for g in range(width // SCAN_W):
        h, m, l = _split3(x[:, g * SCAN_W:(g + 1) * SCAN_W])
        s = _dot(jnp.concatenate([h, m, l], axis=0), ones_bd)
        outs.append(s[0:rows] + s[rows:2 * rows] + s[2 * rows:3 * rows])
    return outs[0] if len(outs) == 1 else jnp.concatenate(outs, axis=1)


def _run_interleaved(gens):
    while gens:
        still = []
        for gen in gens:
            try:
                next(gen)
                still.append(gen)
            except StopIteration:
                pass
        gens = still


def _mod_kernel(cc_ref, w_ref, b_ref, o_ref):
    s = cc_ref[...]
    s = s * _sigmoid(s)
    w = w_ref[...]
    r0 = jnp.sum(s[:, 0:1] * w, axis=0, keepdims=True)
    r1 = jnp.sum(s[:, 1:2] * w, axis=0, keepdims=True)
    o_ref[...] = jnp.concatenate([r0, r1], axis=0) + b_ref[...]


def _mod_call(cc, w_mod, b_mod):
    d, n = w_mod.shape
    tn = 1024
    return pl.pallas_call(
        _mod_kernel,
        grid=(n // tn,),
        in_specs=[pl.BlockSpec((d, 2), lambda j: (0, 0)),
                  pl.BlockSpec((d, tn), lambda j: (0, j)),
                  pl.BlockSpec((1, tn), lambda j: (0, j))],
        out_specs=pl.BlockSpec((2, tn), lambda j: (0, j)),
        out_shape=jax.ShapeDtypeStruct((2, n), F32),
        compiler_params=_cparams(("parallel",)),
        name="mod",
    )(cc, w_mod, b_mod)


def _modulated_rmsnorm(x, g, sc, sh):
    ms = jnp.mean(x * x, axis=-1, keepdims=True)
    return (x * lax.rsqrt(ms + RMS_EPS) * g) * (1.0 + sc) + sh


def _inproj_kernel(x_ref, g_ref, sc_ref, sh_ref, w_ref, o_ref, h_ref):
    @pl.when(pl.program_id(1) == 0)
    def _():
        h_ref[...] = _modulated_rmsnorm(x_ref[...], g_ref[...], sc_ref[...], sh_ref[...]).astype(BF16)

    o_ref[...] = _dot(h_ref[...], w_ref[...]).astype(o_ref.dtype)


def _inproj_call(x, g, sc, sh, w, col_blocks, tm, tn):
    t, d = x.shape
    nj = len(col_blocks)
    if col_blocks == list(range(nj)):
        wmap = lambda i, j: (0, j)
    else:
        k, tail = nj - 1, col_blocks[-1]
        assert col_blocks[:k] == list(range(k))
        wmap = lambda i, j: (0, jnp.where(j < k, j, tail))
    vec = pl.BlockSpec((1, d), lambda i, j: (0, 0))
    return pl.pallas_call(
        _inproj_kernel,
        grid=(t // tm, nj),
        in_specs=[pl.BlockSpec((tm, d), lambda i, j: (i, 0)), vec, vec, vec,
                  pl.BlockSpec((d, tn), wmap)],
        out_specs=pl.BlockSpec((tm, tn), lambda i, j: (i, j)),
        out_shape=jax.ShapeDtypeStruct((t, nj * tn), BF16),
        scratch_shapes=[pltpu.VMEM((tm, d), BF16)],
        compiler_params=_cparams(("parallel", "arbitrary")),
        name="inproj",
    )(x, g, sc, sh, w)


HALO = 16


def _conv3(main, prev_blk, next_blk, w, first, last):
    main = main.astype(F32)
    tm = main.shape[0]
    rows = lax.broadcasted_iota(jnp.int32, main.shape, 0)
    before = jnp.where(first, 0.0, prev_blk[HALO - 1:HALO, :].astype(F32))
    after = jnp.where(last, 0.0, next_blk[0:1, :].astype(F32))
    dn = jnp.where(rows == 0, before, pltpu.roll(main, 1, 0))
    up = jnp.where(rows == tm - 1, after, pltpu.roll(main, tm - 1, 0))
    return w[0:1, :] * dn + w[1:2, :] * main + w[2:3, :] * up


def _store_groups(o_ref, x):
    for g in range(o_ref.shape[0]):
        o_ref[g] = x[:, g * SCAN_W:(g + 1) * SCAN_W].astype(o_ref.dtype)


def _terms_kernel(pr, pr_p, pr_n, pk, pk_p, pk_n, pv, pv_p, pv_n, plo, plo_p, plo_n,
                  cw_r, cw_k, cw_v, cw_l, w0, wdec, a0, wiclr, wgate, k_k, k_a, r_k, ones_bd,
                  r_o, v_o, kk_o, lw_o, kd_o, b_o, g_o, bonus_o):
    i = pl.program_id(0)
    first = i == 0
    last = i == pl.num_programs(0) - 1
    r = _conv3(pr[...], pr_p[...], pr_n[...], cw_r[...], first, last)
    k = _conv3(pk[...], pk_p[...], pk_n[...], cw_k[...], first, last)
    v = _conv3(pv[...], pv_p[...], pv_n[...], cw_v[...], first, last)
    lora = _conv3(plo[...], plo_p[...], plo_n[...], cw_l[...], first, last)
    ones = ones_bd[...]
    _store_groups(r_o, r)
    _store_groups(v_o, v)
    kk = k * k_k[...]
    kk = kk * lax.rsqrt(jnp.maximum(_segsum(kk * kk, ones), 1e-24))
    _store_groups(kk_o, kk)
    gate = _sigmoid(lora[:, 4 * LORA_PAD:4 * LORA_PAD + GATE_LORA]).astype(BF16)
    g_o[...] = _dot(gate, wgate[...]).astype(g_o.dtype)
    ksum = None
    for d in range(2):
        dec = jnp.tanh(lora[:, d * LORA_PAD:(d + 1) * LORA_PAD]).astype(BF16)
        w_pre = w0[d:d + 1, :] + _dot(dec, wdec[d])
        _store_groups(lw_o.at[d], -EXP_M05 * _sigmoid(w_pre))
        icl = lora[:, (2 + d) * LORA_PAD:(3 + d) * LORA_PAD].astype(BF16)
        a = _sigmoid(a0[d:d + 1, :] + _dot(icl, wiclr[d]))
        kd = k * (1.0 + (a - 1.0) * k_a[...])
        _store_groups(kd_o.at[d], kd)
        _store_groups(b_o.at[d], kk * a)
        ksum = kd if ksum is None else ksum + kd
    bonus_o[...] = (_segsum(r * ksum * r_k[...], ones) * v).astype(bonus_o.dtype)


def _terms_call(p, lora_block, cw_r, cw_k, cw_v, cw_l, w0, wdec, a0, wiclr, wgate, k_k, k_a, r_k, ones_bd, tm):
    t = p.shape[0]
    d = k_k.shape[1]
    nhb = t // HALO
    rh = tm // HALO

    def trio(width, cb):
        return [pl.BlockSpec((tm, width), lambda i: (i, cb)),
                pl.BlockSpec((HALO, width), lambda i: (jnp.maximum(i * rh - 1, 0), cb)),
                pl.BlockSpec((HALO, width), lambda i: (jnp.minimum((i + 1) * rh, nhb - 1), cb))]

    def full(a):
        nd = a.ndim
        return pl.BlockSpec(a.shape, lambda i: (0,) * nd)

    consts = [cw_r, cw_k, cw_v, cw_l, w0, wdec, a0, wiclr, wgate, k_k, k_a, r_k, ones_bd]
    ng = d // SCAN_W
    tok = pl.BlockSpec((tm, d), lambda i: (i, 0))
    grp = pl.BlockSpec((ng, tm, SCAN_W), lambda i: (0, i, 0))
    grp2 = pl.BlockSpec((2, ng, tm, SCAN_W), lambda i: (0, 0, i, 0))
    sd = jax.ShapeDtypeStruct((t, d), BF16)
    gsd = jax.ShapeDtypeStruct((ng, t, SCAN_W), BF16)
    gsd2 = jax.ShapeDtypeStruct((2, ng, t, SCAN_W), BF16)
    lw_sd = jax.ShapeDtypeStruct((2, ng, t, SCAN_W), F32)
    return pl.pallas_call(
        _terms_kernel,
        grid=(t // tm,),
        in_specs=trio(d, 0) + trio(d, 1) + trio(d, 2) + trio(LORA_COLS, lora_block) + [full(a) for a in consts],
        out_specs=[grp, grp, grp, grp2, grp2, grp2, tok, tok],
        out_shape=[gsd, gsd, gsd, lw_sd, gsd2, gsd2, sd, sd],
        compiler_params=_cparams(("parallel",)),
        name="terms",
    )(*([p] * 12), *consts)


def _scan_kernel(r_ref, v_ref, kk_ref, lw_ref, kd_ref, b_ref, p0_ref, y_ref, pf_ref,
                 state_ref, rt_s, yl_s, m_s, c_s, *, reverse, nck):
    L, W, G = SCAN_L, SCAN_W, SCAN_G
    i = pl.program_id(1)
    nb = pl.num_programs(1) - 1
    slot_w = i % 2
    slot_r = 1 - slot_w

    @pl.when(i == 0)
    def _():
        state_ref[...] = p0_ref[...]
        rt_s[1] = jnp.zeros(rt_s.shape[1:], rt_s.dtype)
        yl_s[1] = jnp.zeros(yl_s.shape[1:], yl_s.dtype)
        m_s[1] = jnp.zeros(m_s.shape[1:], m_s.dtype)
        c_s[1] = jnp.zeros(c_s.shape[1:], c_s.dtype)

    row = lax.broadcasted_iota(jnp.int32, (W, W), 0)
    col = lax.broadcasted_iota(jnp.int32, (W, W), 1)
    bd = (row >> 6) == (col >> 6)
    eye_w = row == col
    tr = lax.broadcasted_iota(jnp.int32, (L, W), 0)
    tc = lax.broadcasted_iota(jnp.int32, (L, W), 1) & (L - 1)
    t1 = lax.broadcasted_iota(jnp.int32, (L, L), 0)
    t2 = lax.broadcasted_iota(jnp.int32, (L, L), 1)
    if reverse:
        strict, incl, tinc = tc > tr, tc >= tr, t2 >= t1
    else:
        strict, incl, tinc = tc < tr, tc <= tr, t2 <= t1
    tinc = jnp.where(tinc, 1.0, 0.0).astype(BF16)
    same16 = (tr >> 4) == (tc >> 4)
    same32 = (tr >> 5) == (tc >> 5)
    eye_row = jnp.where(tr == tc, 1.0, 0.0)

    def stack(x):
        return jnp.where(bd, jnp.concatenate([x] * G, axis=0), 0.0).astype(BF16)

    def mm(a, b_stacked):
        return _dot(a.astype(BF16), b_stacked)

    def prep(c):
        sl = pl.ds(c * L, L)
        lw = lw_ref[sl, :]
        h, m, l = _split3(lw)
        cs = _dot(tinc, h) + _dot(tinc, m) + _dot(tinc, l)
        yield
        r, v, kk = r_ref[sl, :].astype(F32), v_ref[sl, :].astype(F32), kk_ref[sl, :].astype(F32)
        kd, b = kd_ref[sl, :].astype(F32), b_ref[sl, :].astype(F32)
        tot = jnp.sum(lw, axis=0, keepdims=True)
        kap = kk * jnp.exp(cs - lw)
        rh = r * jnp.exp(cs)
        ginv = jnp.exp(-cs)
        gend = jnp.exp(tot - cs)
        s_bh = stack(b * ginv)
        s_kh = stack(kd * ginv)
        kr16 = jnp.concatenate([kap, rh], axis=0).astype(BF16)
        a_b = _dot_nt(kr16, s_bh)
        a_k = _dot_nt(kr16, s_kh)
        yield
        akb = jnp.where(strict, a_b[0:L], 0.0)
        arb = jnp.where(incl, a_b[L:2 * L], 0.0)
        akk = jnp.where(strict, a_k[0:L], 0.0)
        ark = jnp.where(incl, a_k[L:2 * L], 0.0)
        dg = jnp.where(same16, akb, 0.0)
        s_v = stack(v)
        d2 = mm(dg, stack(dg))
        akkv = mm(akk, s_v)
        yield
        d4 = mm(d2, stack(d2))
        tm_ = mm(eye_row - dg, stack(eye_row + d2))
        yield
        d8 = mm(d4, stack(d4))
        tm_ = mm(tm_, stack(eye_row + d4))
        yield
        tm_ = mm(tm_, stack(eye_row + d8))
        yield
        o1 = jnp.where(same16, 0.0, jnp.where(same32, akb, 0.0))
        x1 = mm(tm_, stack(o1))
        yield
        tm_ = tm_ - mm(x1, stack(tm_))
        yield
        o2 = jnp.where(same32, 0.0, akb)
        x2 = mm(tm_, stack(o2))
        yield
        tm_ = tm_ - mm(x2, stack(tm_))
        yield
        kt = mm(tm_, stack(kap))
        u = mm(tm_, stack(akkv))
        yield
        rt = rh - mm(arb, stack(kt))
        yl = mm(ark, s_v) - mm(arb, stack(u))
        bt16 = (b * gend).astype(BF16)
        kb16 = jnp.concatenate([(kd * gend).astype(BF16), bt16], axis=0)
        vu16 = jnp.concatenate([v, -u], axis=0).astype(BF16)
        m_raw = _dot_tn(bt16, kt.astype(BF16))
        c_raw = _dot_tn(kb16, vu16)
        yield
        m_bd = jnp.where(eye_w, jnp.broadcast_to(jnp.exp(tot), (W, W)), 0.0) - jnp.where(bd, m_raw, 0.0)
        rt_s[slot_w, c] = rt.astype(BF16)
        yl_s[slot_w, c] = yl
        m_s[slot_w, c] = m_bd.astype(BF16)
        c_s[slot_w, c] = jnp.where(bd, c_raw, 0.0)

    def state_pass():
        p = state_ref[...]
        for c in (range(nck - 1, -1, -1) if reverse else range(nck)):
            p16 = p.astype(BF16)
            y_ref[pl.ds(c * L, L), :] = (_dot(rt_s[slot_r, c], p16) + yl_s[slot_r, c]).astype(y_ref.dtype)
            p = jnp.where(i > 0, _dot(m_s[slot_r, c], p16) + c_s[slot_r, c], p)
            yield
        state_ref[...] = p
        pf_ref[...] = p

    @pl.when(i < nb)
    def _():
        _run_interleaved([state_pass()] + [prep(c) for c in range(nck)])

    @pl.when(i == nb)
    def _():
        _run_interleaved([state_pass()])


def _scan_call(r, v, kk, lw, kd, b, p0, d, tb):
    ng, t, _ = r.shape
    nb = t // tb
    reverse = d == 1
    nck = tb // SCAN_L

    def block(step):
        step = jnp.clip(step, 0, nb - 1)
        return nb - 1 - step if reverse else step

    shared = pl.BlockSpec((None, tb, SCAN_W), lambda g, i: (g, block(i), 0))
    per_dir = pl.BlockSpec((None, None, tb, SCAN_W), lambda g, i: (d, g, block(i), 0))
    lagged = pl.BlockSpec((None, tb, SCAN_W), lambda g, i: (g, block(i - 1), 0))
    st = pl.BlockSpec((None, SCAN_W, SCAN_W), lambda g, i: (g, 0, 0))
    return pl.pallas_call(
        functools.partial(_scan_kernel, reverse=reverse, nck=nck),
        grid=(ng, nb + 1),
        in_specs=[shared, shared, shared, per_dir, per_dir, per_dir, st],
        out_specs=[lagged, st],
        out_shape=[jax.ShapeDtypeStruct((ng, t, SCAN_W), BF16), jax.ShapeDtypeStruct((ng, SCAN_W, SCAN_W), F32)],
        scratch_shapes=[pltpu.VMEM((SCAN_W, SCAN_W), F32),
                        pltpu.VMEM((2, nck, SCAN_L, SCAN_W), BF16),
                        pltpu.VMEM((2, nck, SCAN_L, SCAN_W), F32),
                        pltpu.VMEM((2, nck, SCAN_W, SCAN_W), BF16),
                        pltpu.VMEM((2, nck, SCAN_W, SCAN_W), F32)],
        compiler_params=_cparams(("parallel", "arbitrary")),
        name="scan_bwd" if reverse else "scan_fwd",
    )(r, v, kk, lw, kd, b, p0)


def _gelu(x):
    return 0.5 * x * (1.0 + lax.erf(x * 0.7071067811865476))


def _merge_kernel(y0, y1, bonus, g, uv, ga, gb, lnx_g, lnx_b, lnv_g, lnv_b, ws, bsp, wpa, wpb, ones_bd, m_o):
    ones = ones_bd[...]
    inv_n = 1.0 / HEAD
    y = jnp.concatenate([y0[g_].astype(F32) + y1[g_].astype(F32) for g_ in range(y0.shape[0])], axis=1)
    mu = _segsum(y, ones) * inv_n
    yc = y - mu
    var = _segsum(yc * yc, ones) * inv_n
    yn = yc * lax.rsqrt(var + LNX_EPS) * lnx_g[...] + lnx_b[...]
    y_rwkv = ((yn + bonus[...].astype(F32)) * g[...].astype(F32)).astype(BF16)
    branch_a = _dot(y_rwkv, wpa[...])

    z = _gelu(uv[...].astype(F32))
    half = z.shape[1] // 2
    u, vv = z[:, :half], z[:, half:]
    mu = jnp.mean(vv, axis=-1, keepdims=True)
    vc = vv - mu
    var = jnp.mean(vc * vc, axis=-1, keepdims=True)
    vn = (vc * lax.rsqrt(var + LN_EPS) * lnv_g[...] + lnv_b[...]).astype(BF16)
    gch = half // GMLP_GROUPS
    rows = []
    for ch in range(z.shape[0] // CHUNK):
        cols = [_dot(ws[gi], vn[ch * CHUNK:(ch + 1) * CHUNK, gi * gch:(gi + 1) * gch]) for gi in range(GMLP_GROUPS)]
        rows.append(jnp.concatenate(cols, axis=1) + bsp[...])
    s = rows[0] if len(rows) == 1 else jnp.concatenate(rows, axis=0)
    y_gmlp = (u * s).astype(BF16)
    branch_b = _dot(y_gmlp, wpb[...])
    m_o[...] = (_sigmoid(ga[...].astype(F32)) * branch_a + _sigmoid(gb[...].astype(F32)) * branch_b).astype(BF16)


def _merge_call(y0, y1, bonus, g, p, lnx_g, lnx_b, lnv_g, lnv_b, ws, bsp, wpa, wpb, ones_bd, tm):
    t, d = bonus.shape

    def full(a):
        nd = a.ndim
        return pl.BlockSpec(a.shape, lambda i: (0,) * nd, pipeline_mode=pl.Buffered(1))

    tok = pl.BlockSpec((tm, d), lambda i: (i, 0))
    grp = pl.BlockSpec((d // SCAN_W, tm, SCAN_W), lambda i: (0, i, 0))
    consts = [lnx_g, lnx_b, lnv_g, lnv_b, ws, bsp, wpa, wpb, ones_bd]
    return pl.pallas_call(
        _merge_kernel,
        grid=(t // tm,),
        in_specs=[grp, grp, tok, tok,
                  pl.BlockSpec((tm, d), lambda i: (i, 3)),
                  pl.BlockSpec((tm, d), lambda i: (i, 4)),
                  pl.BlockSpec((tm, d), lambda i: (i, 5)),
                  ] + [full(a) for a in consts],
        out_specs=tok,
        out_shape=jax.ShapeDtypeStruct((t, d), BF16),
        compiler_params=_cparams(("parallel",)),
        name="merge",
    )(y0, y1, bonus, g, p, p, p, *consts)


def _outproj_kernel(m, w, x, gt1, g_post1, g_pre2, sc2, sh2, x1_o, h2_o):
    mix = _dot(m[...], w[...])
    ms = jnp.mean(mix * mix, axis=-1, keepdims=True)
    x1 = x[...] + gt1[...] * (mix * lax.rsqrt(ms + RMS_EPS) * g_post1[...])
    x1_o[...] = x1
    h2_o[...] = _modulated_rmsnorm(x1, g_pre2[...], sc2[...], sh2[...]).astype(BF16)


def _outproj_call(m, w, x, gt1, g_post1, g_pre2, sc2, sh2, tm):
    t, d = x.shape
    tok = pl.BlockSpec((tm, d), lambda i: (i, 0))
    vec = pl.BlockSpec((1, d), lambda i: (0, 0))
    return pl.pallas_call(
        _outproj_kernel,
        grid=(t // tm,),
        in_specs=[tok, pl.BlockSpec((d, d), lambda i: (0, 0)), tok, vec, vec, vec, vec, vec],
        out_specs=[tok, tok],
        out_shape=[jax.ShapeDtypeStruct((t, d), F32), jax.ShapeDtypeStruct((t, d), BF16)],
        compiler_params=_cparams(("parallel",)),
        name="outproj",
    )(m, w, x, gt1, g_post1, g_pre2, sc2, sh2)


FFN_CONV_ROWS = 128
FFN_CONV_W = 128


def _matmul_kernel(x_ref, w_ref, o_ref, w16):
    @pl.when(pl.program_id(1) == 0)
    def _():
        w16[...] = w_ref[...].astype(BF16)

    o_ref[...] = _dot(x_ref[...], w16[...]).astype(o_ref.dtype)


def _matmul_call(x, w, tm, tn, out_dtype, name):
    m, k = x.shape
    n = w.shape[1]
    return pl.pallas_call(
        _matmul_kernel,
        grid=(n // tn, m // tm),
        in_specs=[pl.BlockSpec((tm, k), lambda j, i: (i, 0)), pl.BlockSpec((k, tn), lambda j, i: (0, j))],
        out_specs=pl.BlockSpec((tm, tn), lambda j, i: (i, j)),
        out_shape=jax.ShapeDtypeStruct((m, n), out_dtype),
        scratch_shapes=[pltpu.VMEM((k, tn), BF16)],
        compiler_params=_cparams(("parallel", "arbitrary")),
        name=name,
    )(x, w)


def _ffn_tail_kernel(ua_m, ua_p, ua_n, ub_m, ub_p, ub_n, cwa, cwb, wd, x1, gt2, g_post2, o_ref, abuf, bbuf, acc):
    i = pl.program_id(0)
    c = pl.program_id(1)
    tm = ua_m.shape[0]
    fc = ua_m.shape[1]
    first = i == 0
    last = i == pl.num_programs(0) - 1

    @pl.when(c == 0)
    def _():
        acc[...] = jnp.zeros_like(acc)

    def fill(buf, main, prev, nxt):
        buf[0:GRID_W, :] = jnp.where(first, 0.0, prev[...].astype(F32))
        buf[GRID_W:GRID_W + tm, :] = main[...].astype(F32)
        buf[GRID_W + tm:GRID_W + tm + GRID_W, :] = jnp.where(last, 0.0, nxt[...].astype(F32))

    fill(abuf, ua_m, ua_p, ua_n)
    fill(bbuf, ub_m, ub_p, ub_n)

    def conv_piece(u_ref, cw_ref, r0, cols):
        nr = FFN_CONV_ROWS
        cw = cw_ref[:, cols]
        taps = [u_ref[r0 + dr * GRID_W:r0 + dr * GRID_W + nr, cols] for dr in range(3)]
        col_sums = [cw[dc:dc + 1, :] * taps[0] + cw[3 + dc:4 + dc, :] * taps[1] + cw[6 + dc:7 + dc, :] * taps[2]
                    for dc in range(3)]
        gcol = lax.broadcasted_iota(jnp.int32, (nr, FFN_CONV_W), 0) & (GRID_W - 1)
        left = jnp.where(gcol == 0, 0.0, pltpu.roll(col_sums[0], 1, 0))
        right = jnp.where(gcol == GRID_W - 1, 0.0, pltpu.roll(col_sums[2], nr - 1, 0))
        return col_sums[1] + left + right

    for r0 in range(0, tm, FFN_CONV_ROWS):
        pieces = []
        for s in range(fc // FFN_CONV_W):
            cols = slice(s * FFN_CONV_W, (s + 1) * FFN_CONV_W)
            a = conv_piece(abuf, cwa, r0, cols)
            b = conv_piece(bbuf, cwb, r0, cols)
            pieces.append((a * _sigmoid(a) * b).astype(BF16))
        acc[r0:r0 + FFN_CONV_ROWS, :] += _dot(jnp.concatenate(pieces, axis=1), wd[...])

    @pl.when(c == pl.num_programs(1) - 1)
    def _():
        f = acc[...]
        ms = jnp.mean(f * f, axis=-1, keepdims=True)
        o_ref[...] = x1[...] + gt2[...] * (f * lax.rsqrt(ms + RMS_EPS) * g_post2[...])


def _ffn_tail_call(up, cw, w_down, x1, gt2, g_post2, tm, fc):
    t, d = x1.shape
    dff = w_down.shape[0]
    nc = dff // fc
    hb = tm // GRID_W
    nhb = t // GRID_W

    def trio(off):
        return [pl.BlockSpec((tm, fc), lambda i, c: (i, off + c)),
                pl.BlockSpec((GRID_W, fc), lambda i, c: (jnp.maximum(i * hb - 1, 0), off + c)),
                pl.BlockSpec((GRID_W, fc), lambda i, c: (jnp.minimum((i + 1) * hb, nhb - 1), off + c))]

    tok = pl.BlockSpec((tm, d), lambda i, c: (i, 0))
    vec = pl.BlockSpec((1, d), lambda i, c: (0, 0))
    rows_all = tm + 2 * GRID_W
    return pl.pallas_call(
        _ffn_tail_kernel,
        grid=(t // tm, nc),
        in_specs=trio(0) + trio(nc) + [pl.BlockSpec((9, fc), lambda i, c: (0, c)),
                                       pl.BlockSpec((9, fc), lambda i, c: (0, nc + c)),
                                       pl.BlockSpec((fc, d), lambda i, c: (c, 0)),
                                       tok, vec, vec],
        out_specs=tok,
        out_shape=jax.ShapeDtypeStruct((t, d), F32),
        scratch_shapes=[pltpu.VMEM((rows_all, fc), F32), pltpu.VMEM((rows_all, fc), F32), pltpu.VMEM((tm, d), F32)],
        compiler_params=_cparams(("parallel", "arbitrary")),
        name="ffn_tail",
    )(*([up] * 6), cw, cw, w_down, x1, gt2, g_post2)


def _pad_lora(a, axis):
    pad = [(0, 0)] * a.ndim
    pad[axis] = (0, LORA_PAD - a.shape[axis])
    return jnp.pad(a, pad)


def kernel(x, c, ctx, c_ctx, w_mod, b_mod, g_pre1, g_post1, g_pre2, g_post2, w_in, conv_rwkv, w0, w_decay_up, a0, w_iclr_up, w_gate_up, k_k, k_a, r_k, lnx_g, lnx_b, ln_v_g, ln_v_b, w_spatial, b_spatial, w_proj_a, w_proj_b, w_out, w_up, conv_ffn, w_down):
    assert x.shape[0] == 1 and w_mod.shape[0] == 1
    xt, ct = x[0], ctx[0]
    t, d = xt.shape
    tc = ct.shape[0]
    row = lambda a: a.reshape(1, -1)

    off_dec = 3 * d
    off_iclr = off_dec + 2 * DECAY_LORA
    off_gate = off_iclr + 2 * ICLR_LORA
    rwkv_cols = off_gate + GATE_LORA

    def lora_layout(a):
        parts = [_pad_lora(a[:, off_dec + k * DECAY_LORA:off_dec + (k + 1) * DECAY_LORA], 1) for k in range(2)]
        parts += [_pad_lora(a[:, off_iclr + k * ICLR_LORA:off_iclr + (k + 1) * ICLR_LORA], 1) for k in range(2)]
        parts += [a[:, off_gate:rwkv_cols], jnp.zeros((a.shape[0], LORA_COLS - 4 * LORA_PAD - GATE_LORA), a.dtype)]
        return jnp.concatenate(parts, axis=1)

    wi = w_in[0]
    w_in_p = jnp.concatenate([wi[:, :off_dec], wi[:, rwkv_cols:], lora_layout(wi)], axis=1).astype(BF16)
    cr = conv_rwkv[0]
    cw_r, cw_k, cw_v, cw_l = cr[:, :d], cr[:, d:2 * d], cr[:, 2 * d:3 * d], lora_layout(cr)
    wdec = _pad_lora(w_decay_up[0], 1).astype(BF16)
    wiclr = _pad_lora(w_iclr_up[0], 1).astype(BF16)
    wgate = w_gate_up[0].astype(BF16)
    ones_bd = jnp.kron(jnp.eye(SCAN_G, dtype=F32), jnp.ones((HEAD, HEAD), F32)).astype(BF16)
    gch = ln_v_g.shape[1] // GMLP_GROUPS
    bsp = jnp.repeat(b_spatial[0].T, gch, axis=1)
    cw_ffn = conv_ffn[0].reshape(9, -1)

    mod = _mod_call(jnp.stack([c[0], c_ctx], axis=1), w_mod[0], row(b_mod[0]))
    sh1, sc1, gt1, sh2, sc2, gt2 = [mod[0:1, k * d:(k + 1) * d] for k in range(6)]
    csh1, csc1 = mod[1:2, 0:d], mod[1:2, d:2 * d]

    n_blocks = w_in_p.shape[1] // 1024
    p = _inproj_call(xt, row(g_pre1[0]), sc1, sh1, w_in_p, list(range(n_blocks)), min(1024, t), 1024)
    pc = _inproj_call(ct, row(g_pre1[0]), csc1, csh1, w_in_p, list(range(6)) + [n_blocks - 1], min(256, tc), 1024)

    targs = (cw_r, cw_k, cw_v, cw_l, w0[0], wdec, a0[0], wiclr, wgate, row(k_k[0]), row(k_a[0]), row(r_k[0]), ones_bd)
    r_c, v_c, kk_c, lw_c, kd_c, b_c, _, _ = _terms_call(pc, 6, *targs, tm=min(256, tc))
    r_x, v_x, kk_x, lw_x, kd_x, b_x, g_x, bonus_x = _terms_call(p, n_blocks - 1, *targs, tm=256)
    zero_state = jnp.zeros((d // SCAN_W, SCAN_W, SCAN_W), F32)
    ys = []
    for dirn in range(2):
        _, s_ctx = _scan_call(r_c, v_c, kk_c, lw_c, kd_c, b_c, zero_state, dirn, min(256, tc))
        y_d, _ = _scan_call(r_x, v_x, kk_x, lw_x, kd_x, b_x, s_ctx, dirn, min(512, t))
        ys.append(y_d)

    m = _merge_call(ys[0], ys[1], bonus_x, g_x, p, row(lnx_g[0]), row(lnx_b[0]), row(ln_v_g[0]), row(ln_v_b[0]),
                    w_spatial[0].astype(BF16), bsp, w_proj_a[0].astype(BF16), w_proj_b[0].astype(BF16), ones_bd,
                    tm=min(256, t))
    x1, h2 = _outproj_call(m, w_out[0].astype(BF16), xt, gt1, row(g_post1[0]), row(g_pre2[0]), sc2, sh2,
                           tm=min(512, t))

    up = _matmul_call(h2, w_up[0], min(1024, t), 1024, BF16, "ffn_up")
    out = _ffn_tail_call(up, cw_ffn, w_down[0].astype(BF16), x1, gt2, row(g_post2[0]), tm=min(512, t), fc=512)
    return out[None]
```

```python
import functools

import jax
import jax.numpy as jnp
from jax import lax
from jax.experimental import pallas as pl
from jax.experimental.pallas import tpu as pltpu

F32 = jnp.float32
BF16 = jnp.bfloat16

RMS_EPS = 1e-6
LNX_EPS = 64e-5
LN_EPS = 1e-5
HEAD = 64
GRID_W = 64
CHUNK = 128
GMLP_GROUPS = 8
DECAY_LORA = 96
ICLR_LORA = 96
GATE_LORA = 256
LORA_PAD = 128
LORA_COLS = 1024
EXP_M05 = 0.6065306597126334

SCAN_L = 64
SCAN_G = 4
SCAN_W = SCAN_G * HEAD

VMEM_LIMIT = 56 * 1024 * 1024


def _cparams(sem):
    return pltpu.CompilerParams(dimension_semantics=sem, vmem_limit_bytes=VMEM_LIMIT)


def _dot(a, b):
    return jnp.dot(a, b, preferred_element_type=F32)


def _dot_nt(a, b):
    return lax.dot_general(a, b, (((1,), (1,)), ((), ())), preferred_element_type=F32)


def _dot_tn(a, b):
    return lax.dot_general(a, b, (((0,), (0,)), ((), ())), preferred_element_type=F32)


def _sigmoid(x):
    return 1.0 / (1.0 + jnp.exp(-x))


def _split3(x):
    h = x.astype(BF16)
    r1 = x - h.astype(F32)
    m = r1.astype(BF16)
    l = (r1 - m.astype(F32)).astype(BF16)
    return h, m, l


def _segsum(x, ones_bd):
    rows, width = x.shape
    outs = []
    for g in range(width // SCAN_W):
        h, m, l = _split3(x[:, g * SCAN_W:(g + 1) * SCAN_W])
        s = _dot(jnp.concatenate([h, m, l], axis=0), ones_bd)
        outs.append(s[0:rows] + s[rows:2 * rows] + s[2 * rows:3 * rows])
    return outs[0] if len(outs) == 1 else jnp.concatenate(outs, axis=1)


def _run_interleaved(gens):
    while gens:
        still = []
        for gen in gens:
            try:
                next(gen)
                still.append(gen)
            except StopIteration:
                pass
        gens = still


def _mod_kernel(cc_ref, w_ref, b_ref, o_ref):
    s = cc_ref[...]
    s = s * _sigmoid(s)
    w = w_ref[...]
    r0 = jnp.sum(s[:, 0:1] * w, axis=0, keepdims=True)
    r1 = jnp.sum(s[:, 1:2] * w, axis=0, keepdims=True)
    o_ref[...] = jnp.concatenate([r0, r1], axis=0) + b_ref[...]


def _mod_call(cc, w_mod, b_mod):
    d, n = w_mod.shape
    tn = 1024
    return pl.pallas_call(
        _mod_kernel,
        grid=(n // tn,),
        in_specs=[pl.BlockSpec((d, 2), lambda j: (0, 0)),
                  pl.BlockSpec((d, tn), lambda j: (0, j)),
                  pl.BlockSpec((1, tn), lambda j: (0, j))],
        out_specs=pl.BlockSpec((2, tn), lambda j: (0, j)),
        out_shape=jax.ShapeDtypeStruct((2, n), F32),
        compiler_params=_cparams(("parallel",)),
        name="mod",
    )(cc, w_mod, b_mod)


def _modulated_rmsnorm(x, g, sc, sh):
    ms = jnp.mean(x * x, axis=-1, keepdims=True)
    return (x * lax.rsqrt(ms + RMS_EPS) * g) * (1.0 + sc) + sh


def _inproj_kernel(x_ref, g_ref, sc_ref, sh_ref, w_ref, o_ref, h_ref):
    @pl.when(pl.program_id(1) == 0)
    def _():
        h_ref[...] = _modulated_rmsnorm(x_ref[...], g_ref[...], sc_ref[...], sh_ref[...]).astype(BF16)

    o_ref[...] = _dot(h_ref[...], w_ref[...]).astype(o_ref.dtype)


def _inproj_call(x, g, sc, sh, w, col_blocks, tm, tn):
    t, d = x.shape
    nj = len(col_blocks)
    if col_blocks == list(range(nj)):
        wmap = lambda i, j: (0, j)
    else:
        k, tail = nj - 1, col_blocks[-1]
        assert col_blocks[:k] == list(range(k))
        wmap = lambda i, j: (0, jnp.where(j < k, j, tail))
    vec = pl.BlockSpec((1, d), lambda i, j: (0, 0))
    return pl.pallas_call(
        _inproj_kernel,
        grid=(t // tm, nj),
        in_specs=[pl.BlockSpec((tm, d), lambda i, j: (i, 0)), vec, vec, vec,
                  pl.BlockSpec((d, tn), wmap)],
        out_specs=pl.BlockSpec((tm, tn), lambda i, j: (i, j)),
        out_shape=jax.ShapeDtypeStruct((t, nj * tn), BF16),
        scratch_shapes=[pltpu.VMEM((tm, d), BF16)],
        compiler_params=_cparams(("parallel", "arbitrary")),
        name="inproj",
    )(x, g, sc, sh, w)


HALO = 16


def _conv3(main, prev_blk, next_blk, w, first, last):
    main = main.astype(F32)
    tm = main.shape[0]
    rows = lax.broadcasted_iota(jnp.int32, main.shape, 0)
    before = jnp.where(first, 0.0, prev_blk[HALO - 1:HALO, :].astype(F32))
    after = jnp.where(last, 0.0, next_blk[0:1, :].astype(F32))
    dn = jnp.where(rows == 0, before, pltpu.roll(main, 1, 0))
    up = jnp.where(rows == tm - 1, after, pltpu.roll(main, tm - 1, 0))
    return w[0:1, :] * dn + w[1:2, :] * main + w[2:3, :] * up


def _store_groups(o_ref, x):
    for g in range(o_ref.shape[0]):
        o_ref[g] = x[:, g * SCAN_W:(g + 1) * SCAN_W].astype(o_ref.dtype)


def _terms_kernel(pr, pr_p, pr_n, pk, pk_p, pk_n, pv, pv_p, pv_n, plo, plo_p, plo_n,
                  cw_r, cw_k, cw_v, cw_l, w0, wdec, a0, wiclr, wgate, k_k, k_a, r_k, ones_bd,
                  r_o, v_o, kk_o, lw_o, kd_o, b_o, g_o, bonus_o):
    i = pl.program_id(0)
    first = i == 0
    last = i == pl.num_programs(0) - 1
    r = _conv3(pr[...], pr_p[...], pr_n[...], cw_r[...], first, last)
    k = _conv3(pk[...], pk_p[...], pk_n[...], cw_k[...], first, last)
    v = _conv3(pv[...], pv_p[...], pv_n[...], cw_v[...], first, last)
    lora = _conv3(plo[...], plo_p[...], plo_n[...], cw_l[...], first, last)
    ones = ones_bd[...]
    _store_groups(r_o, r)
    _store_groups(v_o, v)
    kk = k * k_k[...]
    kk = kk * lax.rsqrt(jnp.maximum(_segsum(kk * kk, ones), 1e-24))
    _store_groups(kk_o, kk)
    gate = _sigmoid(lora[:, 4 * LORA_PAD:4 * LORA_PAD + GATE_LORA]).astype(BF16)
    g_o[...] = _dot(gate, wgate[...]).astype(g_o.dtype)
    ksum = None
    for d in range(2):
        dec = jnp.tanh(lora[:, d * LORA_PAD:(d + 1) * LORA_PAD]).astype(BF16)
        w_pre = w0[d:d + 1, :] + _dot(dec, wdec[d])
        _store_groups(lw_o.at[d], -EXP_M05 * _sigmoid(w_pre))
        icl = lora[:, (2 + d) * LORA_PAD:(3 + d) * LORA_PAD].astype(BF16)
        a = _sigmoid(a0[d:d + 1, :] + _dot(icl, wiclr[d]))
        kd = k * (1.0 + (a - 1.0) * k_a[...])
        _store_groups(kd_o.at[d], kd)
        _store_groups(b_o.at[d], kk * a)
        ksum = kd if ksum is None else ksum + kd
    bonus_o[...] = (_segsum(r * ksum * r_k[...], ones) * v).astype(bonus_o.dtype)


def _terms_call(p, lora_block, cw_r, cw_k, cw_v, cw_l, w0, wdec, a0, wiclr, wgate, k_k, k_a, r_k, ones_bd, tm):
    t = p.shape[0]
    d = k_k.shape[1]
    nhb = t // HALO
    rh = tm // HALO

    def trio(width, cb):
        return [pl.BlockSpec((tm, width), lambda i: (i, cb)),
                pl.BlockSpec((HALO, width), lambda i: (jnp.maximum(i * rh - 1, 0), cb)),
                pl.BlockSpec((HALO, width), lambda i: (jnp.minimum((i + 1) * rh, nhb - 1), cb))]

    def full(a):
        nd = a.ndim
        return pl.BlockSpec(a.shape, lambda i: (0,) * nd)

    consts = [cw_r, cw_k, cw_v, cw_l, w0, wdec, a0, wiclr, wgate, k_k, k_a, r_k, ones_bd]
    ng = d // SCAN_W
    tok = pl.BlockSpec((tm, d), lambda i: (i, 0))
    grp = pl.BlockSpec((ng, tm, SCAN_W), lambda i: (0, i, 0))
    grp2 = pl.BlockSpec((2, ng, tm, SCAN_W), lambda i: (0, 0, i, 0))
    sd = jax.ShapeDtypeStruct((t, d), BF16)
    gsd = jax.ShapeDtypeStruct((ng, t, SCAN_W), BF16)
    gsd2 = jax.ShapeDtypeStruct((2, ng, t, SCAN_W), BF16)
    lw_sd = jax.ShapeDtypeStruct((2, ng, t, SCAN_W), F32)
    return pl.pallas_call(
        _terms_kernel,
        grid=(t // tm,),
        in_specs=trio(d, 0) + trio(d, 1) + trio(d, 2) + trio(LORA_COLS, lora_block) + [full(a) for a in consts],
        out_specs=[grp, grp, grp, grp2, grp2, grp2, tok, tok],
        out_shape=[gsd, gsd, gsd, lw_sd, gsd2, gsd2, sd, sd],
        compiler_params=_cparams(("parallel",)),
        name="terms",
    )(*([p] * 12), *consts)


def _scan_kernel(r_ref, v_ref, kk_ref, lw_ref, kd_ref, b_ref, p0_ref, y_ref, pf_ref,
                 state_ref, rm_s, yl_s, c_s, *, reverse, nck):
    L, W, G = SCAN_L, SCAN_W, SCAN_G
    i = pl.program_id(1)
    nb = pl.num_programs(1) - 1
    slot_w = i % 2
    slot_r = 1 - slot_w

    @pl.when(i == 0)
    def _():
        state_ref[...] = p0_ref[...]
        rm_s[1] = jnp.zeros(rm_s.shape[1:], rm_s.dtype)
        yl_s[1] = jnp.zeros(yl_s.shape[1:], yl_s.dtype)
        c_s[1] = jnp.zeros(c_s.shape[1:], c_s.dtype)

    row = lax.broadcasted_iota(jnp.int32, (W, W), 0)
    col = lax.broadcasted_iota(jnp.int32, (W, W), 1)
    bd = (row >> 6) == (col >> 6)
    eye_w = row == col
    tr = lax.broadcasted_iota(jnp.int32, (L, W), 0)
    tc = lax.broadcasted_iota(jnp.int32, (L, W), 1) & (L - 1)
    t1 = lax.broadcasted_iota(jnp.int32, (L, L), 0)
    t2 = lax.broadcasted_iota(jnp.int32, (L, L), 1)
    if reverse:
        strict, incl, tinc = tc > tr, tc >= tr, t2 >= t1
    else:
        strict, incl, tinc = tc < tr, tc <= tr, t2 <= t1
    tinc = jnp.where(tinc, 1.0, 0.0).astype(BF16)
    same16 = (tr >> 4) == (tc >> 4)
    same32 = (tr >> 5) == (tc >> 5)
    eye_row = jnp.where(tr == tc, 1.0, 0.0)

    def stack(x):
        return jnp.where(bd, jnp.concatenate([x] * G, axis=0), 0.0).astype(BF16)

    def mm(a, b_stacked):
        return _dot(a.astype(BF16), b_stacked)

    def prep(c):
        sl = pl.ds(c * L, L)
        lw = lw_ref[sl, :]
        h, m, l = _split3(lw)
        cs = _dot(tinc, h) + _dot(tinc, m) + _dot(tinc, l)
        yield
        r, v, kk = r_ref[sl, :].astype(F32), v_ref[sl, :].astype(F32), kk_ref[sl, :].astype(F32)
        kd, b = kd_ref[sl, :].astype(F32), b_ref[sl, :].astype(F32)
        tot = jnp.sum(lw, axis=0, keepdims=True)
        kap = kk * jnp.exp(cs - lw)
        rh = r * jnp.exp(cs)
        ginv = jnp.exp(-cs)
        gend = jnp.exp(tot - cs)
        s_bh = stack(b * ginv)
        s_kh = stack(kd * ginv)
        kr16 = jnp.concatenate([kap, rh], axis=0).astype(BF16)
        a_b = _dot_nt(kr16, s_bh)
        a_k = _dot_nt(kr16, s_kh)
        yield
        akb = jnp.where(strict, a_b[0:L], 0.0)
        arb = jnp.where(incl, a_b[L:2 * L], 0.0)
        akk = jnp.where(strict, a_k[0:L], 0.0)
        ark = jnp.where(incl, a_k[L:2 * L], 0.0)
        dg = jnp.where(same16, akb, 0.0)
        s_v = stack(v)
        d2 = mm(dg, stack(dg))
        av = mm(jnp.concatenate([akk, ark], axis=0), s_v)
        akkv, arkv = av[0:L], av[L:2 * L]
        yield
        tm_ = eye_row - dg
        both = mm(jnp.concatenate([d2, tm_], axis=0), stack(d2))
        d4, tm_ = both[0:L], tm_ + both[L:2 * L]
        yield
        both = mm(jnp.concatenate([d4, tm_], axis=0), stack(d4))
        d8, tm_ = both[0:L], tm_ + both[L:2 * L]
        yield
        tm_ = tm_ + mm(tm_, stack(d8))
        yield
        o1 = jnp.where(same16, 0.0, jnp.where(same32, akb, 0.0))
        x1 = mm(tm_, stack(o1))
        yield
        tm_ = tm_ - mm(x1, stack(tm_))
        yield
        o2 = jnp.where(same32, 0.0, akb)
        x2 = mm(tm_, stack(o2))
        yield
        tm_ = tm_ - mm(x2, stack(tm_))
        yield
        kt = mm(tm_, stack(kap))
        u = mm(tm_, stack(akkv))
        yield
        rt = rh - mm(arb, stack(kt))
        yl = arkv - mm(arb, stack(u))
        bt16 = (b * gend).astype(BF16)
        kb16 = jnp.concatenate([(kd * gend).astype(BF16), bt16], axis=0)
        vu16 = jnp.concatenate([v, -u], axis=0).astype(BF16)
        m_raw = _dot_tn(bt16, kt.astype(BF16))
        c_raw = _dot_tn(kb16, vu16)
        yield
        m_bd = jnp.where(eye_w, jnp.broadcast_to(jnp.exp(tot), (W, W)), 0.0) - jnp.where(bd, m_raw, 0.0)
        rm_s[slot_w, c, 0:L] = rt.astype(BF16)
        rm_s[slot_w, c, L:L + W] = m_bd.astype(BF16)
        yl_s[slot_w, c] = yl
        c_s[slot_w, c] = jnp.where(bd, c_raw, 0.0)

    def state_pass():
        p = state_ref[...]
        for c in (range(nck - 1, -1, -1) if reverse else range(nck)):
            rp = _dot(rm_s[slot_r, c], p.astype(BF16))
            y_ref[pl.ds(c * L, L), :] = (rp[0:L] + yl_s[slot_r, c]).astype(y_ref.dtype)
            p = jnp.where(i > 0, rp[L:L + W] + c_s[slot_r, c], p)
            yield
        state_ref[...] = p
        pf_ref[...] = p

    @pl.when(i < nb)
    def _():
        _run_interleaved([state_pass()] + [prep(c) for c in range(nck)])

    @pl.when(i == nb)
    def _():
        _run_interleaved([state_pass()])


def _scan_call(r, v, kk, lw, kd, b, p0, d, tb):
    ng, t, _ = r.shape
    nb = t // tb
    reverse = d == 1
    nck = tb // SCAN_L

    def block(step):
        step = jnp.clip(step, 0, nb - 1)
        return nb - 1 - step if reverse else step

    shared = pl.BlockSpec((None, tb, SCAN_W), lambda g, i: (g, block(i), 0))
    per_dir = pl.BlockSpec((None, None, tb, SCAN_W), lambda g, i: (d, g, block(i), 0))
    lagged = pl.BlockSpec((None, tb, SCAN_W), lambda g, i: (g, block(i - 1), 0))
    st = pl.BlockSpec((None, SCAN_W, SCAN_W), lambda g, i: (g, 0, 0))
    return pl.pallas_call(
        functools.partial(_scan_kernel, reverse=reverse, nck=nck),
        grid=(ng, nb + 1),
        in_specs=[shared, shared, shared, per_dir, per_dir, per_dir, st],
        out_specs=[lagged, st],
        out_shape=[jax.ShapeDtypeStruct((ng, t, SCAN_W), BF16), jax.ShapeDtypeStruct((ng, SCAN_W, SCAN_W), F32)],
        scratch_shapes=[pltpu.VMEM((SCAN_W, SCAN_W), F32),
                        pltpu.VMEM((2, nck, SCAN_L + SCAN_W, SCAN_W), BF16),
                        pltpu.VMEM((2, nck, SCAN_L, SCAN_W), F32),
                        pltpu.VMEM((2, nck, SCAN_W, SCAN_W), F32)],
        compiler_params=_cparams(("parallel", "arbitrary")),
        name="scan_bwd" if reverse else "scan_fwd",
    )(r, v, kk, lw, kd, b, p0)


def _gelu(x):
    return 0.5 * x * (1.0 + lax.erf(x * 0.7071067811865476))


def _merge_kernel(y0, y1, bonus, g, uv, ga, gb, lnx_g, lnx_b, lnv_g, lnv_b, ws, bsp, wpa, wpb, ones_bd, m_o):
    ones = ones_bd[...]
    inv_n = 1.0 / HEAD
    y = jnp.concatenate([y0[g_].astype(F32) + y1[g_].astype(F32) for g_ in range(y0.shape[0])], axis=1)
    mu = _segsum(y, ones) * inv_n
    yc = y - mu
    var = _segsum(yc * yc, ones) * inv_n
    yn = yc * lax.rsqrt(var + LNX_EPS) * lnx_g[...] + lnx_b[...]
    y_rwkv = ((yn + bonus[...].astype(F32)) * g[...].astype(F32)).astype(BF16)
    branch_a = _dot(y_rwkv, wpa[...])

    z = _gelu(uv[...].astype(F32))
    half = z.shape[1] // 2
    u, vv = z[:, :half], z[:, half:]
    mu = jnp.mean(vv, axis=-1, keepdims=True)
    vc = vv - mu
    var = jnp.mean(vc * vc, axis=-1, keepdims=True)
    vn = (vc * lax.rsqrt(var + LN_EPS) * lnv_g[...] + lnv_b[...]).astype(BF16)
    gch = half // GMLP_GROUPS
    rows = []
    for ch in range(z.shape[0] // CHUNK):
        cols = [_dot(ws[gi], vn[ch * CHUNK:(ch + 1) * CHUNK, gi * gch:(gi + 1) * gch]) for gi in range(GMLP_GROUPS)]
        rows.append(jnp.concatenate(cols, axis=1) + bsp[...])
    s = rows[0] if len(rows) == 1 else jnp.concatenate(rows, axis=0)
    y_gmlp = (u * s).astype(BF16)
    branch_b = _dot(y_gmlp, wpb[...])
    m_o[...] = (_sigmoid(ga[...].astype(F32)) * branch_a + _sigmoid(gb[...].astype(F32)) * branch_b).astype(BF16)


def _merge_call(y0, y1, bonus, g, p, lnx_g, lnx_b, lnv_g, lnv_b, ws, bsp, wpa, wpb, ones_bd, tm):
    t, d = bonus.shape

    def full(a):
        nd = a.ndim
        return pl.BlockSpec(a.shape, lambda i: (0,) * nd, pipeline_mode=pl.Buffered(1))

    tok = pl.BlockSpec((tm, d), lambda i: (i, 0))
    grp = pl.BlockSpec((d // SCAN_W, tm, SCAN_W), lambda i: (0, i, 0))
    consts = [lnx_g, lnx_b, lnv_g, lnv_b, ws, bsp, wpa, wpb, ones_bd]
    return pl.pallas_call(
        _merge_kernel,
        grid=(t // tm,),
        in_specs=[grp, grp, tok, tok,
                  pl.BlockSpec((tm, d), lambda i: (i, 3)),
                  pl.BlockSpec((tm, d), lambda i: (i, 4)),
                  pl.BlockSpec((tm, d), lambda i: (i, 5)),
                  ] + [full(a) for a in consts],
        out_specs=tok,
        out_shape=jax.ShapeDtypeStruct((t, d), BF16),
        compiler_params=_cparams(("parallel",)),
        name="merge",
    )(y0, y1, bonus, g, p, p, p, *consts)


def _outproj_kernel(m, w, x, gt1, g_post1, g_pre2, sc2, sh2, x1_o, h2_o):
    mix = _dot(m[...], w[...])
    ms = jnp.mean(mix * mix, axis=-1, keepdims=True)
    x1 = x[...] + gt1[...] * (mix * lax.rsqrt(ms + RMS_EPS) * g_post1[...])
    x1_o[...] = x1
    h2_o[...] = _modulated_rmsnorm(x1, g_pre2[...], sc2[...], sh2[...]).astype(BF16)


def _outproj_call(m, w, x, gt1, g_post1, g_pre2, sc2, sh2, tm):
    t, d = x.shape
    tok = pl.BlockSpec((tm, d), lambda i: (i, 0))
    vec = pl.BlockSpec((1, d), lambda i: (0, 0))
    return pl.pallas_call(
        _outproj_kernel,
        grid=(t // tm,),
        in_specs=[tok, pl.BlockSpec((d, d), lambda i: (0, 0)), tok, vec, vec, vec, vec, vec],
        out_specs=[tok, tok],
        out_shape=[jax.ShapeDtypeStruct((t, d), F32), jax.ShapeDtypeStruct((t, d), BF16)],
        compiler_params=_cparams(("parallel",)),
        name="outproj",
    )(m, w, x, gt1, g_post1, g_pre2, sc2, sh2)


FFN_CONV_ROWS = 128
FFN_CONV_W = 128


def _matmul_kernel(x_ref, w_ref, o_ref, w16):
    @pl.when(pl.program_id(1) == 0)
    def _():
        w16[...] = w_ref[...].astype(BF16)

    o_ref[...] = _dot(x_ref[...], w16[...]).astype(o_ref.dtype)


def _matmul_call(x, w, tm, tn, out_dtype, name):
    m, k = x.shape
    n = w.shape[1]
    return pl.pallas_call(
        _matmul_kernel,
        grid=(n // tn, m // tm),
        in_specs=[pl.BlockSpec((tm, k), lambda j, i: (i, 0)), pl.BlockSpec((k, tn), lambda j, i: (0, j))],
        out_specs=pl.BlockSpec((tm, tn), lambda j, i: (i, j)),
        out_shape=jax.ShapeDtypeStruct((m, n), out_dtype),
        scratch_shapes=[pltpu.VMEM((k, tn), BF16)],
        compiler_params=_cparams(("parallel", "arbitrary")),
        name=name,
    )(x, w)


def _ffn_tail_kernel(ua_m, ua_p, ua_n, ub_m, ub_p, ub_n, cwa, cwb, wd, x1, gt2, g_post2, o_ref, abuf, bbuf, acc):
    i = pl.program_id(0)
    c = pl.program_id(1)
    tm = ua_m.shape[0]
    fc = ua_m.shape[1]
    first = i == 0
    last = i == pl.num_programs(0) - 1

    @pl.when(c == 0)
    def _():
        acc[...] = jnp.zeros_like(acc)

    def fill(buf, main, prev, nxt):
        zero = jnp.zeros(prev.shape, prev.dtype)
        buf[0:GRID_W, :] = jnp.where(first, zero, prev[...])
        buf[GRID_W:GRID_W + tm, :] = main[...]
        buf[GRID_W + tm:GRID_W + tm + GRID_W, :] = jnp.where(last, zero, nxt[...])

    fill(abuf, ua_m, ua_p, ua_n)
    fill(bbuf, ub_m, ub_p, ub_n)

    def conv_piece(u_ref, cw_ref, r0, cols):
        nr = FFN_CONV_ROWS
        cw = cw_ref[:, cols].astype(BF16)
        taps = [u_ref[r0 + dr * GRID_W:r0 + dr * GRID_W + nr, cols] for dr in range(3)]
        col_sums = [(cw[dc:dc + 1, :] * taps[0] + cw[3 + dc:4 + dc, :] * taps[1]
                     + cw[6 + dc:7 + dc, :] * taps[2]).astype(F32) for dc in range(3)]
        gcol = lax.broadcasted_iota(jnp.int32, (nr, FFN_CONV_W), 0) & (GRID_W - 1)
        left = jnp.where(gcol == 0, 0.0, pltpu.roll(col_sums[0], 1, 0))
        right = jnp.where(gcol == GRID_W - 1, 0.0, pltpu.roll(col_sums[2], nr - 1, 0))
        return col_sums[1] + left + right

    for r0 in range(0, tm, FFN_CONV_ROWS):
        pieces = []
        for s in range(fc // FFN_CONV_W):
            cols = slice(s * FFN_CONV_W, (s + 1) * FFN_CONV_W)
            a = conv_piece(abuf, cwa, r0, cols)
            b = conv_piece(bbuf, cwb, r0, cols)
            pieces.append((a * _sigmoid(a) * b).astype(BF16))
        acc[r0:r0 + FFN_CONV_ROWS, :] += _dot(jnp.concatenate(pieces, axis=1), wd[...])

    @pl.when(c == pl.num_programs(1) - 1)
    def _():
        f = acc[...]
        ms = jnp.mean(f * f, axis=-1, keepdims=True)
        o_ref[...] = x1[...] + gt2[...] * (f * lax.rsqrt(ms + RMS_EPS) * g_post2[...])


def _ffn_tail_call(up, cw, w_down, x1, gt2, g_post2, tm, fc):
    t, d = x1.shape
    dff = w_down.shape[0]
    nc = dff // fc
    hb = tm // GRID_W
    nhb = t // GRID_W

    def trio(off):
        return [pl.BlockSpec((tm, fc), lambda i, c: (i, off + c)),
                pl.BlockSpec((GRID_W, fc), lambda i, c: (jnp.maximum(i * hb - 1, 0), off + c)),
                pl.BlockSpec((GRID_W, fc), lambda i, c: (jnp.minimum((i + 1) * hb, nhb - 1), off + c))]

    tok = pl.BlockSpec((tm, d), lambda i, c: (i, 0))
    vec = pl.BlockSpec((1, d), lambda i, c: (0, 0))
    rows_all = tm + 2 * GRID_W
    return pl.pallas_call(
        _ffn_tail_kernel,
        grid=(t // tm, nc),
        in_specs=trio(0) + trio(nc) + [pl.BlockSpec((9, fc), lambda i, c: (0, c)),
                                       pl.BlockSpec((9, fc), lambda i, c: (0, nc + c)),
                                       pl.BlockSpec((fc, d), lambda i, c: (c, 0)),
                                       tok, vec, vec],
        out_specs=tok,
        out_shape=jax.ShapeDtypeStruct((t, d), F32),
        scratch_shapes=[pltpu.VMEM((rows_all, fc), BF16), pltpu.VMEM((rows_all, fc), BF16), pltpu.VMEM((tm, d), F32)],
        compiler_params=_cparams(("parallel", "arbitrary")),
        name="ffn_tail",
    )(*([up] * 6), cw, cw, w_down, x1, gt2, g_post2)


def _pad_lora(a, axis):
    pad = [(0, 0)] * a.ndim
    pad[axis] = (0, LORA_PAD - a.shape[axis])
    return jnp.pad(a, pad)


def kernel(x, c, ctx, c_ctx, w_mod, b_mod, g_pre1, g_post1, g_pre2, g_post2, w_in, conv_rwkv, w0, w_decay_up, a0, w_iclr_up, w_gate_up, k_k, k_a, r_k, lnx_g, lnx_b, ln_v_g, ln_v_b, w_spatial, b_spatial, w_proj_a, w_proj_b, w_out, w_up, conv_ffn, w_down):
    assert x.shape[0] == 1 and w_mod.shape[0] == 1
    xt, ct = x[0], ctx[0]
    t, d = xt.shape
    tc = ct.shape[0]
    row = lambda a: a.reshape(1, -1)

    off_dec = 3 * d
    off_iclr = off_dec + 2 * DECAY_LORA
    off_gate = off_iclr + 2 * ICLR_LORA
    rwkv_cols = off_gate + GATE_LORA

    def lora_layout(a):
        parts = [_pad_lora(a[:, off_dec + k * DECAY_LORA:off_dec + (k + 1) * DECAY_LORA], 1) for k in range(2)]
        parts += [_pad_lora(a[:, off_iclr + k * ICLR_LORA:off_iclr + (k + 1) * ICLR_LORA], 1) for k in range(2)]
        parts += [a[:, off_gate:rwkv_cols], jnp.zeros((a.shape[0], LORA_COLS - 4 * LORA_PAD - GATE_LORA), a.dtype)]
        return jnp.concatenate(parts, axis=1)

    wi = w_in[0]
    w_in_p = jnp.concatenate([wi[:, :off_dec], wi[:, rwkv_cols:], lora_layout(wi)], axis=1).astype(BF16)
    cr = conv_rwkv[0]
    cw_r, cw_k, cw_v, cw_l = cr[:, :d], cr[:, d:2 * d], cr[:, 2 * d:3 * d], lora_layout(cr)
    wdec = _pad_lora(w_decay_up[0], 1).astype(BF16)
    wiclr = _pad_lora(w_iclr_up[0], 1).astype(BF16)
    wgate = w_gate_up[0].astype(BF16)
    ones_bd = jnp.kron(jnp.eye(SCAN_G, dtype=F32), jnp.ones((HEAD, HEAD), F32)).astype(BF16)
    gch = ln_v_g.shape[1] // GMLP_GROUPS
    bsp = jnp.repeat(b_spatial[0].T, gch, axis=1)
    cw_ffn = conv_ffn[0].reshape(9, -1)

    mod = _mod_call(jnp.stack([c[0], c_ctx], axis=1), w_mod[0], row(b_mod[0]))
    sh1, sc1, gt1, sh2, sc2, gt2 = [mod[0:1, k * d:(k + 1) * d] for k in range(6)]
    csh1, csc1 = mod[1:2, 0:d], mod[1:2, d:2 * d]

    n_blocks = w_in_p.shape[1] // 1024
    p = _inproj_call(xt, row(g_pre1[0]), sc1, sh1, w_in_p, list(range(n_blocks)), min(1024, t), 1024)
    pc = _inproj_call(ct, row(g_pre1[0]), csc1, csh1, w_in_p, list(range(6)) + [n_blocks - 1], min(256, tc), 1024)

    targs = (cw_r, cw_k, cw_v, cw_l, w0[0], wdec, a0[0], wiclr, wgate, row(k_k[0]), row(k_a[0]), row(r_k[0]), ones_bd)
    r_c, v_c, kk_c, lw_c, kd_c, b_c, _, _ = _terms_call(pc, 6, *targs, tm=min(256, tc))
    r_x, v_x, kk_x, lw_x, kd_x, b_x, g_x, bonus_x = _terms_call(p, n_blocks - 1, *targs, tm=256)
    zero_state = jnp.zeros((d // SCAN_W, SCAN_W, SCAN_W), F32)
    ys = []
    for dirn in range(2):
        _, s_ctx = _scan_call(r_c, v_c, kk_c, lw_c, kd_c, b_c, zero_state, dirn, min(256, tc))
        y_d, _ = _scan_call(r_x, v_x, kk_x, lw_x, kd_x, b_x, s_ctx, dirn, min(512, t))
        ys.append(y_d)

    m = _merge_call(ys[0], ys[1], bonus_x, g_x, p, row(lnx_g[0]), row(lnx_b[0]), row(ln_v_g[0]), row(ln_v_b[0]),
                    w_spatial[0].astype(BF16), bsp, w_proj_a[0].astype(BF16), w_proj_b[0].astype(BF16), ones_bd,
                    tm=min(256, t))
    x1, h2 = _outproj_call(m, w_out[0].astype(BF16), xt, gt1, row(g_post1[0]), row(g_pre2[0]), sc2, sh2,
                           tm=min(512, t))

    up = _matmul_call(h2, w_up[0], min(1024, t), 1024, BF16, "ffn_up")
    out = _ffn_tail_call(up, cw_ffn, w_down[0].astype(BF16), x1, gt2, row(g_post2[0]), tm=min(512, t), fc=512)
    return out[None]
```

```python
import functools

import jax
import jax.numpy as jnp
from jax import lax
from jax.experimental import pallas as pl
from jax.experimental.pallas import tpu as pltpu

F32 = jnp.float32
BF16 = jnp.bfloat16

RMS_EPS = 1e-6
LNX_EPS = 64e-5
LN_EPS = 1e-5
HEAD = 64
GRID_W = 64
CHUNK = 128
GMLP_GROUPS = 8
DECAY_LORA = 96
ICLR_LORA = 96
GATE_LORA = 256
LORA_PAD = 128
LORA_COLS = 1024
EXP_M05 = 0.6065306597126334

SCAN_L = 64
SCAN_G = 4
SCAN_W = SCAN_G * HEAD

VMEM_LIMIT = 56 * 1024 * 1024


def _cparams(sem):
    return pltpu.CompilerParams(dimension_semantics=sem, vmem_limit_bytes=VMEM_LIMIT)


def _dot(a, b):
    return jnp.dot(a, b, preferred_element_type=F32)


def _dot_nt(a, b):
    return lax.dot_general(a, b, (((1,), (1,)), ((), ())), preferred_element_type=F32)


def _dot_tn(a, b):
    return lax.dot_general(a, b, (((0,), (0,)), ((), ())), preferred_element_type=F32)


def _sigmoid(x):
    return 1.0 / (1.0 + jnp.exp(-x))


def _split3(x):
    h = x.astype(BF16)
    r1 = x - h.astype(F32)
    m = r1.astype(BF16)
    l = (r1 - m.astype(F32)).astype(BF16)
    return h, m, l


def _segsum(x, ones_bd):
    rows, width = x.shape
    outs = []
    for g in range(width // SCAN_W):
        h, m, l = _split3(x[:, g * SCAN_W:(g + 1) * SCAN_W])
        s = _dot(jnp.concatenate([h, m, l], axis=0), ones_bd)
        outs.append(s[0:rows] + s[rows:2 * rows] + s[2 * rows:3 * rows])
    return outs[0] if len(outs) == 1 else jnp.concatenate(outs, axis=1)


def _run_interleaved(gens):
    while gens:
        still = []
        for gen in gens:
            try:
                next(gen)
                still.append(gen)
            except StopIteration:
                pass
        gens = still


def _mod_kernel(cc_ref, w_ref, b_ref, o_ref):
    s = cc_ref[...]
    s = s * _sigmoid(s)
    w = w_ref[...]
    r0 = jnp.sum(s[:, 0:1] * w, axis=0, keepdims=True)
    r1 = jnp.sum(s[:, 1:2] * w, axis=0, keepdims=True)
    o_ref[...] = jnp.concatenate([r0, r1], axis=0) + b_ref[...]


def _mod_call(cc, w_mod, b_mod):
    d, n = w_mod.shape
    tn = 1024
    return pl.pallas_call(
        _mod_kernel,
        grid=(n // tn,),
        in_specs=[pl.BlockSpec((d, 2), lambda j: (0, 0)),
                  pl.BlockSpec((d, tn), lambda j: (0, j)),
                  pl.BlockSpec((1, tn), lambda j: (0, j))],
        out_specs=pl.BlockSpec((2, tn), lambda j: (0, j)),
        out_shape=jax.ShapeDtypeStruct((2, n), F32),
        compiler_params=_cparams(("parallel",)),
        name="mod",
    )(cc, w_mod, b_mod)


def _modulated_rmsnorm(x, g, sc, sh):
    ms = jnp.mean(x * x, axis=-1, keepdims=True)
    return (x * lax.rsqrt(ms + RMS_EPS) * g) * (1.0 + sc) + sh


def _inproj_kernel(x_ref, g_ref, sc_ref, sh_ref, w_ref, o_ref, h_ref):
    @pl.when(pl.program_id(1) == 0)
    def _():
        h_ref[...] = _modulated_rmsnorm(x_ref[...], g_ref[...], sc_ref[...], sh_ref[...]).astype(BF16)

    o_ref[...] = _dot(h_ref[...], w_ref[...]).astype(o_ref.dtype)


def _inproj_call(x, g, sc, sh, w, col_blocks, tm, tn):
    t, d = x.shape
    nj = len(col_blocks)
    if col_blocks == list(range(nj)):
        wmap = lambda i, j: (0, j)
    else:
        k, tail = nj - 1, col_blocks[-1]
        assert col_blocks[:k] == list(range(k))
        wmap = lambda i, j: (0, jnp.where(j < k, j, tail))
    vec = pl.BlockSpec((1, d), lambda i, j: (0, 0))
    return pl.pallas_call(
        _inproj_kernel,
        grid=(t // tm, nj),
        in_specs=[pl.BlockSpec((tm, d), lambda i, j: (i, 0)), vec, vec, vec,
                  pl.BlockSpec((d, tn), wmap)],
        out_specs=pl.BlockSpec((tm, tn), lambda i, j: (i, j)),
        out_shape=jax.ShapeDtypeStruct((t, nj * tn), BF16),
        scratch_shapes=[pltpu.VMEM((tm, d), BF16)],
        compiler_params=_cparams(("parallel", "arbitrary")),
        name="inproj",
    )(x, g, sc, sh, w)


HALO = 16


def _conv3(main, prev_blk, next_blk, w, first, last):
    main = main.astype(F32)
    tm = main.shape[0]
    rows = lax.broadcasted_iota(jnp.int32, main.shape, 0)
    before = jnp.where(first, 0.0, prev_blk[HALO - 1:HALO, :].astype(F32))
    after = jnp.where(last, 0.0, next_blk[0:1, :].astype(F32))
    dn = jnp.where(rows == 0, before, pltpu.roll(main, 1, 0))
    up = jnp.where(rows == tm - 1, after, pltpu.roll(main, tm - 1, 0))
    return w[0:1, :] * dn + w[1:2, :] * main + w[2:3, :] * up


def _store_groups(o_ref, x):
    for g in range(o_ref.shape[0]):
        o_ref[g] = x[:, g * SCAN_W:(g + 1) * SCAN_W].astype(o_ref.dtype)


def _terms_kernel(pr, pr_p, pr_n, pk, pk_p, pk_n, pv, pv_p, pv_n, plo, plo_p, plo_n,
                  cw_r, cw_k, cw_v, cw_l, w0, wdec, a0, wiclr, wgate, k_k, k_a, r_k, ones_bd,
                  r_o, v_o, kk_o, lw_o, kd_o, b_o, g_o, bonus_o):
    i = pl.program_id(0)
    first = i == 0
    last = i == pl.num_programs(0) - 1
    r = _conv3(pr[...], pr_p[...], pr_n[...], cw_r[...], first, last)
    k = _conv3(pk[...], pk_p[...], pk_n[...], cw_k[...], first, last)
    v = _conv3(pv[...], pv_p[...], pv_n[...], cw_v[...], first, last)
    lora = _conv3(plo[...], plo_p[...], plo_n[...], cw_l[...], first, last)
    ones = ones_bd[...]
    _store_groups(r_o, r)
    _store_groups(v_o, v)
    kk = k * k_k[...]
    kk = kk * lax.rsqrt(jnp.maximum(_segsum(kk * kk, ones), 1e-24))
    _store_groups(kk_o, kk)
    gate = _sigmoid(lora[:, 4 * LORA_PAD:4 * LORA_PAD + GATE_LORA]).astype(BF16)
    g_o[...] = _dot(gate, wgate[...]).astype(g_o.dtype)
    ksum = None
    for d in range(2):
        dec = jnp.tanh(lora[:, d * LORA_PAD:(d + 1) * LORA_PAD]).astype(BF16)
        w_pre = w0[d:d + 1, :] + _dot(dec, wdec[d])
        _store_groups(lw_o.at[d], -EXP_M05 * _sigmoid(w_pre))
        icl = lora[:, (2 + d) * LORA_PAD:(3 + d) * LORA_PAD].astype(BF16)
        a = _sigmoid(a0[d:d + 1, :] + _dot(icl, wiclr[d]))
        kd = k * (1.0 + (a - 1.0) * k_a[...])
        _store_groups(kd_o.at[d], kd)
        _store_groups(b_o.at[d], kk * a)
        ksum = kd if ksum is None else ksum + kd
    bonus_o[...] = (_segsum(r * ksum * r_k[...], ones) * v).astype(bonus_o.dtype)


def _terms_call(p, lora_block, cw_r, cw_k, cw_v, cw_l, w0, wdec, a0, wiclr, wgate, k_k, k_a, r_k, ones_bd, tm):
    t = p.shape[0]
    d = k_k.shape[1]
    nhb = t // HALO
    rh = tm // HALO

    def trio(width, cb):
        return [pl.BlockSpec((tm, width), lambda i: (i, cb)),
                pl.BlockSpec((HALO, width), lambda i: (jnp.maximum(i * rh - 1, 0), cb)),
                pl.BlockSpec((HALO, width), lambda i: (jnp.minimum((i + 1) * rh, nhb - 1), cb))]

    def full(a):
        nd = a.ndim
        return pl.BlockSpec(a.shape, lambda i: (0,) * nd)

    consts = [cw_r, cw_k, cw_v, cw_l, w0, wdec, a0, wiclr, wgate, k_k, k_a, r_k, ones_bd]
    ng = d // SCAN_W
    tok = pl.BlockSpec((tm, d), lambda i: (i, 0))
    grp = pl.BlockSpec((ng, tm, SCAN_W), lambda i: (0, i, 0))
    grp2 = pl.BlockSpec((2, ng, tm, SCAN_W), lambda i: (0, 0, i, 0))
    sd = jax.ShapeDtypeStruct((t, d), BF16)
    gsd = jax.ShapeDtypeStruct((ng, t, SCAN_W), BF16)
    gsd2 = jax.ShapeDtypeStruct((2, ng, t, SCAN_W), BF16)
    lw_sd = jax.ShapeDtypeStruct((2, ng, t, SCAN_W), F32)
    return pl.pallas_call(
        _terms_kernel,
        grid=(t // tm,),
        in_specs=trio(d, 0) + trio(d, 1) + trio(d, 2) + trio(LORA_COLS, lora_block) + [full(a) for a in consts],
        out_specs=[grp, grp, grp, grp2, grp2, grp2, tok, tok],
        out_shape=[gsd, gsd, gsd, lw_sd, gsd2, gsd2, sd, sd],
        compiler_params=_cparams(("parallel",)),
        name="terms",
    )(*([p] * 12), *consts)


def _scan_kernel(r_ref, v_ref, kk_ref, lw_ref, kd_ref, b_ref, p0_ref, y_ref, pf_ref,
                 state_ref, rm_s, yl_s, c_s, *, reverse, nck):
    L, W, G = SCAN_L, SCAN_W, SCAN_G
    i = pl.program_id(1)
    nb = pl.num_programs(1) - 1
    slot_w = i % 2
    slot_r = 1 - slot_w

    @pl.when(i == 0)
    def _():
        state_ref[...] = p0_ref[...]
        rm_s[1] = jnp.zeros(rm_s.shape[1:], rm_s.dtype)
        yl_s[1] = jnp.zeros(yl_s.shape[1:], yl_s.dtype)
        c_s[1] = jnp.zeros(c_s.shape[1:], c_s.dtype)

    row = lax.broadcasted_iota(jnp.int32, (W, W), 0)
    col = lax.broadcasted_iota(jnp.int32, (W, W), 1)
    bd = (row >> 6) == (col >> 6)
    eye_w = row == col
    tr = lax.broadcasted_iota(jnp.int32, (L, W), 0)
    tc = lax.broadcasted_iota(jnp.int32, (L, W), 1) & (L - 1)
    t1 = lax.broadcasted_iota(jnp.int32, (L, L), 0)
    t2 = lax.broadcasted_iota(jnp.int32, (L, L), 1)
    if reverse:
        strict, incl, tinc = tc > tr, tc >= tr, t2 >= t1
    else:
        strict, incl, tinc = tc < tr, tc <= tr, t2 <= t1
    tinc = jnp.where(tinc, 1.0, 0.0).astype(BF16)
    same16 = (tr >> 4) == (tc >> 4)
    same32 = (tr >> 5) == (tc >> 5)
    eye_row = jnp.where(tr == tc, 1.0, 0.0)

    def stack(x):
        return jnp.where(bd, jnp.concatenate([x] * G, axis=0), 0.0).astype(BF16)

    def mm(a, b_stacked):
        return _dot(a.astype(BF16), b_stacked)

    def prep(c):
        sl = pl.ds(c * L, L)
        lw = lw_ref[sl, :]
        h, m, l = _split3(lw)
        cs = _dot(tinc, h) + _dot(tinc, m) + _dot(tinc, l)
        yield
        r, v, kk = r_ref[sl, :].astype(F32), v_ref[sl, :].astype(F32), kk_ref[sl, :].astype(F32)
        kd, b = kd_ref[sl, :].astype(F32), b_ref[sl, :].astype(F32)
        tot = jnp.sum(lw, axis=0, keepdims=True)
        kap = kk * jnp.exp(cs - lw)
        rh = r * jnp.exp(cs)
        ginv = jnp.exp(-cs)
        gend = jnp.exp(tot - cs)
        s_bh = stack(b * ginv)
        s_kh = stack(kd * ginv)
        kr16 = jnp.concatenate([kap, rh], axis=0).astype(BF16)
        a_b = _dot_nt(kr16, s_bh)
        a_k = _dot_nt(kr16, s_kh)
        yield
        akb = jnp.where(strict, a_b[0:L], 0.0)
        arb = jnp.where(incl, a_b[L:2 * L], 0.0)
        akk = jnp.where(strict, a_k[0:L], 0.0)
        ark = jnp.where(incl, a_k[L:2 * L], 0.0)
        dg = jnp.where(same16, akb, 0.0)
        s_v = stack(v)
        d2 = mm(dg, stack(dg))
        av = mm(jnp.concatenate([akk, ark], axis=0), s_v)
        akkv, arkv = av[0:L], av[L:2 * L]
        yield
        tm_ = eye_row - dg
        both = mm(jnp.concatenate([d2, tm_], axis=0), stack(d2))
        d4, tm_ = both[0:L], tm_ + both[L:2 * L]
        yield
        both = mm(jnp.concatenate([d4, tm_], axis=0), stack(d4))
        d8, tm_ = both[0:L], tm_ + both[L:2 * L]
        yield
        tm_ = tm_ + mm(tm_, stack(d8))
        yield
        o1 = jnp.where(same16, 0.0, jnp.where(same32, akb, 0.0))
        x1 = mm(tm_, stack(o1))
        yield
        tm_ = tm_ - mm(x1, stack(tm_))
        yield
        o2 = jnp.where(same32, 0.0, akb)
        x2 = mm(tm_, stack(o2))
        yield
        tm_ = tm_ - mm(x2, stack(tm_))
        yield
        kt = mm(tm_, stack(kap))
        u = mm(tm_, stack(akkv))
        yield
        rt = rh - mm(arb, stack(kt))
        yl = arkv - mm(arb, stack(u))
        bt16 = (b * gend).astype(BF16)
        kb16 = jnp.concatenate([(kd * gend).astype(BF16), bt16], axis=0)
        vu16 = jnp.concatenate([v, -u], axis=0).astype(BF16)
        m_raw = _dot_tn(bt16, kt.astype(BF16))
        c_raw = _dot_tn(kb16, vu16)
        yield
        m_bd = jnp.where(eye_w, jnp.broadcast_to(jnp.exp(tot), (W, W)), 0.0) - jnp.where(bd, m_raw, 0.0)
        rm_s[slot_w, c, 0:L] = rt.astype(BF16)
        rm_s[slot_w, c, L:L + W] = m_bd.astype(BF16)
        yl_s[slot_w, c] = yl
        c_s[slot_w, c] = jnp.where(bd, c_raw, 0.0)

    def state_pass():
        p = state_ref[...]
        for c in (range(nck - 1, -1, -1) if reverse else range(nck)):
            rp = _dot(rm_s[slot_r, c], p.astype(BF16))
            y_ref[pl.ds(c * L, L), :] = (rp[0:L] + yl_s[slot_r, c]).astype(y_ref.dtype)
            p = jnp.where(i > 0, rp[L:L + W] + c_s[slot_r, c], p)
            yield
        state_ref[...] = p
        pf_ref[...] = p

    @pl.when(i < nb)
    def _():
        _run_interleaved([state_pass()] + [prep(c) for c in range(nck)])

    @pl.when(i == nb)
    def _():
        _run_interleaved([state_pass()])


def _scan_call(r, v, kk, lw, kd, b, p0, d, tb):
    ng, t, _ = r.shape
    nb = t // tb
    reverse = d == 1
    nck = tb // SCAN_L

    def block(step):
        step = jnp.clip(step, 0, nb - 1)
        return nb - 1 - step if reverse else step

    shared = pl.BlockSpec((None, tb, SCAN_W), lambda g, i: (g, block(i), 0))
    per_dir = pl.BlockSpec((None, None, tb, SCAN_W), lambda g, i: (d, g, block(i), 0))
    lagged = pl.BlockSpec((None, tb, SCAN_W), lambda g, i: (g, block(i - 1), 0))
    st = pl.BlockSpec((None, SCAN_W, SCAN_W), lambda g, i: (g, 0, 0))
    return pl.pallas_call(
        functools.partial(_scan_kernel, reverse=reverse, nck=nck),
        grid=(ng, nb + 1),
        in_specs=[shared, shared, shared, per_dir, per_dir, per_dir, st],
        out_specs=[lagged, st],
        out_shape=[jax.ShapeDtypeStruct((ng, t, SCAN_W), BF16), jax.ShapeDtypeStruct((ng, SCAN_W, SCAN_W), F32)],
        scratch_shapes=[pltpu.VMEM((SCAN_W, SCAN_W), F32),
                        pltpu.VMEM((2, nck, SCAN_L + SCAN_W, SCAN_W), BF16),
                        pltpu.VMEM((2, nck, SCAN_L, SCAN_W), F32),
                        pltpu.VMEM((2, nck, SCAN_W, SCAN_W), F32)],
        compiler_params=_cparams(("parallel", "arbitrary")),
        name="scan_bwd" if reverse else "scan_fwd",
    )(r, v, kk, lw, kd, b, p0)


def _gelu(x):
    return 0.5 * x * (1.0 + lax.erf(x * 0.7071067811865476))


def _merge_kernel(y0, y1, bonus, g, uv, ga, gb, lnx_g, lnx_b, lnv_g, lnv_b, ws, bsp, wpa, wpb, ones_bd, m_o):
    ones = ones_bd[...]
    inv_n = 1.0 / HEAD
    y = jnp.concatenate([y0[g_].astype(F32) + y1[g_].astype(F32) for g_ in range(y0.shape[0])], axis=1)
    mu = _segsum(y, ones) * inv_n
    yc = y - mu
    var = _segsum(yc * yc, ones) * inv_n
    yn = yc * lax.rsqrt(var + LNX_EPS) * lnx_g[...] + lnx_b[...]
    y_rwkv = ((yn + bonus[...].astype(F32)) * g[...].astype(F32)).astype(BF16)
    branch_a = _dot(y_rwkv, wpa[...])

    z = _gelu(uv[...].astype(F32))
    half = z.shape[1] // 2
    u, vv = z[:, :half], z[:, half:]
    mu = jnp.mean(vv, axis=-1, keepdims=True)
    vc = vv - mu
    var = jnp.mean(vc * vc, axis=-1, keepdims=True)
    vn = (vc * lax.rsqrt(var + LN_EPS) * lnv_g[...] + lnv_b[...]).astype(BF16)
    gch = half // GMLP_GROUPS
    rows = []
    for ch in range(z.shape[0] // CHUNK):
        cols = [_dot(ws[gi], vn[ch * CHUNK:(ch + 1) * CHUNK, gi * gch:(gi + 1) * gch]) for gi in range(GMLP_GROUPS)]
        rows.append(jnp.concatenate(cols, axis=1) + bsp[...])
    s = rows[0] if len(rows) == 1 else jnp.concatenate(rows, axis=0)
    y_gmlp = (u * s).astype(BF16)
    branch_b = _dot(y_gmlp, wpb[...])
    m_o[...] = (_sigmoid(ga[...].astype(F32)) * branch_a + _sigmoid(gb[...].astype(F32)) * branch_b).astype(BF16)


def _merge_call(y0, y1, bonus, g, p, lnx_g, lnx_b, lnv_g, lnv_b, ws, bsp, wpa, wpb, ones_bd, tm):
    t, d = bonus.shape

    def full(a):
        nd = a.ndim
        return pl.BlockSpec(a.shape, lambda i: (0,) * nd, pipeline_mode=pl.Buffered(1))

    tok = pl.BlockSpec((tm, d), lambda i: (i, 0))
    grp = pl.BlockSpec((d // SCAN_W, tm, SCAN_W), lambda i: (0, i, 0))
    consts = [lnx_g, lnx_b, lnv_g, lnv_b, ws, bsp, wpa, wpb, ones_bd]
    return pl.pallas_call(
        _merge_kernel,
        grid=(t // tm,),
        in_specs=[grp, grp, tok, tok,
                  pl.BlockSpec((tm, d), lambda i: (i, 3)),
                  pl.BlockSpec((tm, d), lambda i: (i, 4)),
                  pl.BlockSpec((tm, d), lambda i: (i, 5)),
                  ] + [full(a) for a in consts],
        out_specs=tok,
        out_shape=jax.ShapeDtypeStruct((t, d), BF16),
        compiler_params=_cparams(("parallel",)),
        name="merge",
    )(y0, y1, bonus, g, p, p, p, *consts)


def _outproj_kernel(m, w, x, gt1, g_post1, g_pre2, sc2, sh2, x1_o, h2_o):
    mix = _dot(m[...], w[...])
    ms = jnp.mean(mix * mix, axis=-1, keepdims=True)
    x1 = x[...] + gt1[...] * (mix * lax.rsqrt(ms + RMS_EPS) * g_post1[...])
    x1_o[...] = x1
    h2_o[...] = _modulated_rmsnorm(x1, g_pre2[...], sc2[...], sh2[...]).astype(BF16)


def _outproj_call(m, w, x, gt1, g_post1, g_pre2, sc2, sh2, tm):
    t, d = x.shape
    tok = pl.BlockSpec((tm, d), lambda i: (i, 0))
    vec = pl.BlockSpec((1, d), lambda i: (0, 0))
    return pl.pallas_call(
        _outproj_kernel,
        grid=(t // tm,),
        in_specs=[tok, pl.BlockSpec((d, d), lambda i: (0, 0)), tok, vec, vec, vec, vec, vec],
        out_specs=[tok, tok],
        out_shape=[jax.ShapeDtypeStruct((t, d), F32), jax.ShapeDtypeStruct((t, d), BF16)],
        compiler_params=_cparams(("parallel",)),
        name="outproj",
    )(m, w, x, gt1, g_post1, g_pre2, sc2, sh2)


FFN_CONV_ROWS = 128
FFN_CONV_W = 128
FFN_DOWN_ROWS = 512


def _matmul_kernel(x_ref, w_ref, o_ref, w16):
    @pl.when(pl.program_id(1) == 0)
    def _():
        w16[...] = w_ref[...].astype(BF16)

    o_ref[...] = _dot(x_ref[...], w16[...]).astype(o_ref.dtype)


def _matmul_call(x, w, tm, tn, out_dtype, name):
    m, k = x.shape
    n = w.shape[1]
    return pl.pallas_call(
        _matmul_kernel,
        grid=(n // tn, m // tm),
        in_specs=[pl.BlockSpec((tm, k), lambda j, i: (i, 0)), pl.BlockSpec((k, tn), lambda j, i: (0, j))],
        out_specs=pl.BlockSpec((tm, tn), lambda j, i: (i, j)),
        out_shape=jax.ShapeDtypeStruct((m, n), out_dtype),
        scratch_shapes=[pltpu.VMEM((k, tn), BF16)],
        compiler_params=_cparams(("parallel", "arbitrary")),
        name=name,
    )(x, w)


def _ffn_tail_kernel(ua_m, ua_p, ua_n, ub_m, ub_p, ub_n, cwa, cwb, wd, x1, gt2, g_post2, o_ref, abuf, bbuf, acc):
    i = pl.program_id(0)
    c = pl.program_id(1)
    tm = ua_m.shape[0]
    fc = ua_m.shape[1]
    first = i == 0
    last = i == pl.num_programs(0) - 1

    @pl.when(c == 0)
    def _():
        acc[...] = jnp.zeros_like(acc)

    def fill(buf, main, prev, nxt):
        zero = jnp.zeros(prev.shape, prev.dtype)
        buf[0:GRID_W, :] = jnp.where(first, zero, prev[...])
        buf[GRID_W:GRID_W + tm, :] = main[...]
        buf[GRID_W + tm:GRID_W + tm + GRID_W, :] = jnp.where(last, zero, nxt[...])

    fill(abuf, ua_m, ua_p, ua_n)
    fill(bbuf, ub_m, ub_p, ub_n)

    nr = FFN_CONV_ROWS
    rt_ = lax.broadcasted_iota(jnp.int32, (nr, 3 * nr), 0)
    ct_ = lax.broadcasted_iota(jnp.int32, (nr, 3 * nr), 1)
    gcol = rt_ & (GRID_W - 1)
    hit = ((ct_ == rt_ - 1) & (gcol != 0)) | (ct_ == rt_ + nr) | ((ct_ == rt_ + 2 * nr + 1) & (gcol != GRID_W - 1))
    shift_add = jnp.where(hit, 1.0, 0.0).astype(BF16)

    def conv_rows(u_ref, cw_ref, r0):
        cw = cw_ref[...].astype(BF16)
        taps = [u_ref[r0 + dr * GRID_W:r0 + dr * GRID_W + nr, :] for dr in range(3)]
        col_sums = [cw[dc:dc + 1, :] * taps[0] + cw[3 + dc:4 + dc, :] * taps[1] + cw[6 + dc:7 + dc, :] * taps[2]
                    for dc in range(3)]
        return _dot(shift_add, jnp.concatenate(col_sums, axis=0))

    for m0 in range(0, tm, FFN_DOWN_ROWS):
        row_blocks = []
        for r0 in range(m0, m0 + FFN_DOWN_ROWS, nr):
            a = conv_rows(abuf, cwa, r0)
            b = conv_rows(bbuf, cwb, r0)
            row_blocks.append((a * _sigmoid(a) * b).astype(BF16))
        acc[m0:m0 + FFN_DOWN_ROWS, :] += _dot(jnp.concatenate(row_blocks, axis=0), wd[...])

    @pl.when(c == pl.num_programs(1) - 1)
    def _():
        f = acc[...]
        ms = jnp.mean(f * f, axis=-1, keepdims=True)
        o_ref[...] = x1[...] + gt2[...] * (f * lax.rsqrt(ms + RMS_EPS) * g_post2[...])


def _ffn_tail_call(up, cw, w_down, x1, gt2, g_post2, tm, fc):
    t, d = x1.shape
    dff = w_down.shape[0]
    nc = dff // fc
    hb = tm // GRID_W
    nhb = t // GRID_W

    def trio(off):
        return [pl.BlockSpec((tm, fc), lambda i, c: (i, off + c)),
                pl.BlockSpec((GRID_W, fc), lambda i, c: (jnp.maximum(i * hb - 1, 0), off + c)),
                pl.BlockSpec((GRID_W, fc), lambda i, c: (jnp.minimum((i + 1) * hb, nhb - 1), off + c))]

    tok = pl.BlockSpec((tm, d), lambda i, c: (i, 0))
    vec = pl.BlockSpec((1, d), lambda i, c: (0, 0))
    rows_all = tm + 2 * GRID_W
    return pl.pallas_call(
        _ffn_tail_kernel,
        grid=(t // tm, nc),
        in_specs=trio(0) + trio(nc) + [pl.BlockSpec((9, fc), lambda i, c: (0, c)),
                                       pl.BlockSpec((9, fc), lambda i, c: (0, nc + c)),
                                       pl.BlockSpec((fc, d), lambda i, c: (c, 0)),
                                       tok, vec, vec],
        out_specs=tok,
        out_shape=jax.ShapeDtypeStruct((t, d), F32),
        scratch_shapes=[pltpu.VMEM((rows_all, fc), BF16), pltpu.VMEM((rows_all, fc), BF16), pltpu.VMEM((tm, d), F32)],
        compiler_params=_cparams(("parallel", "arbitrary")),
        name="ffn_tail",
    )(*([up] * 6), cw, cw, w_down, x1, gt2, g_post2)


def _pad_lora(a, axis):
    pad = [(0, 0)] * a.ndim
    pad[axis] = (0, LORA_PAD - a.shape[axis])
    return jnp.pad(a, pad)


def kernel(x, c, ctx, c_ctx, w_mod, b_mod, g_pre1, g_post1, g_pre2, g_post2, w_in, conv_rwkv, w0, w_decay_up, a0, w_iclr_up, w_gate_up, k_k, k_a, r_k, lnx_g, lnx_b, ln_v_g, ln_v_b, w_spatial, b_spatial, w_proj_a, w_proj_b, w_out, w_up, conv_ffn, w_down):
    assert x.shape[0] == 1 and w_mod.shape[0] == 1
    xt, ct = x[0], ctx[0]
    t, d = xt.shape
    tc = ct.shape[0]
    row = lambda a: a.reshape(1, -1)

    off_dec = 3 * d
    off_iclr = off_dec + 2 * DECAY_LORA
    off_gate = off_iclr + 2 * ICLR_LORA
    rwkv_cols = off_gate + GATE_LORA

    def lora_layout(a):
        parts = [_pad_lora(a[:, off_dec + k * DECAY_LORA:off_dec + (k + 1) * DECAY_LORA], 1) for k in range(2)]
        parts += [_pad_lora(a[:, off_iclr + k * ICLR_LORA:off_iclr + (k + 1) * ICLR_LORA], 1) for k in range(2)]
        parts += [a[:, off_gate:rwkv_cols], jnp.zeros((a.shape[0], LORA_COLS - 4 * LORA_PAD - GATE_LORA), a.dtype)]
        return jnp.concatenate(parts, axis=1)

    wi = w_in[0]
    w_in_p = jnp.concatenate([wi[:, :off_dec], wi[:, rwkv_cols:], lora_layout(wi)], axis=1).astype(BF16)
    cr = conv_rwkv[0]
    cw_r, cw_k, cw_v, cw_l = cr[:, :d], cr[:, d:2 * d], cr[:, 2 * d:3 * d], lora_layout(cr)
    wdec = _pad_lora(w_decay_up[0], 1).astype(BF16)
    wiclr = _pad_lora(w_iclr_up[0], 1).astype(BF16)
    wgate = w_gate_up[0].astype(BF16)
    ones_bd = jnp.kron(jnp.eye(SCAN_G, dtype=F32), jnp.ones((HEAD, HEAD), F32)).astype(BF16)
    gch = ln_v_g.shape[1] // GMLP_GROUPS
    bsp = jnp.repeat(b_spatial[0].T, gch, axis=1)
    cw_ffn = conv_ffn[0].reshape(9, -1)

    mod = _mod_call(jnp.stack([c[0], c_ctx], axis=1), w_mod[0], row(b_mod[0]))
    sh1, sc1, gt1, sh2, sc2, gt2 = [mod[0:1, k * d:(k + 1) * d] for k in range(6)]
    csh1, csc1 = mod[1:2, 0:d], mod[1:2, d:2 * d]

    n_blocks = w_in_p.shape[1] // 1024
    p = _inproj_call(xt, row(g_pre1[0]), sc1, sh1, w_in_p, list(range(n_blocks)), min(1024, t), 1024)
    pc = _inproj_call(ct, row(g_pre1[0]), csc1, csh1, w_in_p, list(range(6)) + [n_blocks - 1], min(256, tc), 1024)

    targs = (cw_r, cw_k, cw_v, cw_l, w0[0], wdec, a0[0], wiclr, wgate, row(k_k[0]), row(k_a[0]), row(r_k[0]), ones_bd)
    r_c, v_c, kk_c, lw_c, kd_c, b_c, _, _ = _terms_call(pc, 6, *targs, tm=min(256, tc))
    r_x, v_x, kk_x, lw_x, kd_x, b_x, g_x, bonus_x = _terms_call(p, n_blocks - 1, *targs, tm=256)
    zero_state = jnp.zeros((d // SCAN_W, SCAN_W, SCAN_W), F32)
    ys = []
    for dirn in range(2):
        _, s_ctx = _scan_call(r_c, v_c, kk_c, lw_c, kd_c, b_c, zero_state, dirn, min(256, tc))
        y_d, _ = _scan_call(r_x, v_x, kk_x, lw_x, kd_x, b_x, s_ctx, dirn, min(512, t))
        ys.append(y_d)

    m = _merge_call(ys[0], ys[1], bonus_x, g_x, p, row(lnx_g[0]), row(lnx_b[0]), row(ln_v_g[0]), row(ln_v_b[0]),
                    w_spatial[0].astype(BF16), bsp, w_proj_a[0].astype(BF16), w_proj_b[0].astype(BF16), ones_bd,
                    tm=min(256, t))
    x1, h2 = _outproj_call(m, w_out[0].astype(BF16), xt, gt1, row(g_post1[0]), row(g_pre2[0]), sc2, sh2,
                           tm=min(512, t))

    up = _matmul_call(h2, w_up[0], min(2048, t), 1024, BF16, "ffn_up")
    out = _ffn_tail_call(up, cw_ffn, w_down[0].astype(BF16), x1, gt2, row(g_post2[0]), tm=min(512, t), fc=512)
    return out[None]
```

```python
import functools

import jax
import jax.numpy as jnp
from jax import lax
from jax.experimental import pallas as pl
from jax.experimental.pallas import tpu as pltpu

F32 = jnp.float32
BF16 = jnp.bfloat16

RMS_EPS = 1e-6
LNX_EPS = 64e-5
LN_EPS = 1e-5
HEAD = 64
GRID_W = 64
CHUNK = 128
GMLP_GROUPS = 8
DECAY_LORA = 96
ICLR_LORA = 96
GATE_LORA = 256
LORA_PAD = 128
LORA_COLS = 1024
EXP_M05 = 0.6065306597126334

SCAN_L = 64
SCAN_G = 4
SCAN_W = SCAN_G * HEAD

VMEM_LIMIT = 56 * 1024 * 1024


def _cparams(sem):
    return pltpu.CompilerParams(dimension_semantics=sem, vmem_limit_bytes=VMEM_LIMIT)


def _dot(a, b):
    return jnp.dot(a, b, preferred_element_type=F32)


def _dot_nt(a, b):
    return lax.dot_general(a, b, (((1,), (1,)), ((), ())), preferred_element_type=F32)


def _dot_tn(a, b):
    return lax.dot_general(a, b, (((0,), (0,)), ((), ())), preferred_element_type=F32)


def _sigmoid(x):
    return 1.0 / (1.0 + jnp.exp(-x))


def _split3(x):
    h = x.astype(BF16)
    r1 = x - h.astype(F32)
    m = r1.astype(BF16)
    l = (r1 - m.astype(F32)).astype(BF16)
    return h, m, l


def _segsum(x, ones_bd):
    rows, width = x.shape
    outs = []
    for g in range(width // SCAN_W):
        xg = x[:, g * SCAN_W:(g + 1) * SCAN_W]
        h = xg.astype(BF16)
        l = (xg - h.astype(F32)).astype(BF16)
        s = _dot(jnp.concatenate([h, l], axis=0), ones_bd)
        outs.append(s[0:rows] + s[rows:2 * rows])
    return outs[0] if len(outs) == 1 else jnp.concatenate(outs, axis=1)


def _run_interleaved(gens):
    while gens:
        still = []
        for gen in gens:
            try:
                next(gen)
                still.append(gen)
            except StopIteration:
                pass
        gens = still


def _mod_kernel(cc_ref, w_ref, b_ref, o_ref):
    s = cc_ref[...]
    s = s * _sigmoid(s)
    w = w_ref[...]
    r0 = jnp.sum(s[:, 0:1] * w, axis=0, keepdims=True)
    r1 = jnp.sum(s[:, 1:2] * w, axis=0, keepdims=True)
    o_ref[...] = jnp.concatenate([r0, r1], axis=0) + b_ref[...]


def _mod_call(cc, w_mod, b_mod):
    d, n = w_mod.shape
    tn = 1024
    return pl.pallas_call(
        _mod_kernel,
        grid=(n // tn,),
        in_specs=[pl.BlockSpec((d, 2), lambda j: (0, 0)),
                  pl.BlockSpec((d, tn), lambda j: (0, j)),
                  pl.BlockSpec((1, tn), lambda j: (0, j))],
        out_specs=pl.BlockSpec((2, tn), lambda j: (0, j)),
        out_shape=jax.ShapeDtypeStruct((2, n), F32),
        compiler_params=_cparams(("parallel",)),
        name="mod",
    )(cc, w_mod, b_mod)


def _modulated_rmsnorm(x, g, sc, sh):
    ms = jnp.mean(x * x, axis=-1, keepdims=True)
    return (x * lax.rsqrt(ms + RMS_EPS) * g) * (1.0 + sc) + sh


def _inproj_kernel(x_ref, g_ref, sc_ref, sh_ref, w_ref, o_ref, h_ref):
    @pl.when(pl.program_id(1) == 0)
    def _():
        h_ref[...] = _modulated_rmsnorm(x_ref[...], g_ref[...], sc_ref[...], sh_ref[...]).astype(BF16)

    o_ref[...] = _dot(h_ref[...], w_ref[...]).astype(o_ref.dtype)


def _inproj_call(x, g, sc, sh, w, col_blocks, tm, tn):
    t, d = x.shape
    nj = len(col_blocks)
    if col_blocks == list(range(nj)):
        wmap = lambda i, j: (0, j)
    else:
        k, tail = nj - 1, col_blocks[-1]
        assert col_blocks[:k] == list(range(k))
        wmap = lambda i, j: (0, jnp.where(j < k, j, tail))
    vec = pl.BlockSpec((1, d), lambda i, j: (0, 0))
    return pl.pallas_call(
        _inproj_kernel,
        grid=(t // tm, nj),
        in_specs=[pl.BlockSpec((tm, d), lambda i, j: (i, 0)), vec, vec, vec,
                  pl.BlockSpec((d, tn), wmap)],
        out_specs=pl.BlockSpec((tm, tn), lambda i, j: (i, j)),
        out_shape=jax.ShapeDtypeStruct((t, nj * tn), BF16),
        scratch_shapes=[pltpu.VMEM((tm, d), BF16)],
        compiler_params=_cparams(("parallel", "arbitrary")),
        name="inproj",
    )(x, g, sc, sh, w)


HALO = 16


def _conv3(main, prev_blk, next_blk, w, first, last):
    main = main.astype(F32)
    tm = main.shape[0]
    rows = lax.broadcasted_iota(jnp.int32, (8, main.shape[1]), 0)
    before = jnp.where(first, 0.0, prev_blk[HALO - 1:HALO, :].astype(F32))
    after = jnp.where(last, 0.0, next_blk[0:1, :].astype(F32))
    dn = pltpu.roll(main, 1, 0)
    dn = jnp.concatenate([jnp.where(rows == 0, before, dn[0:8]), dn[8:]], axis=0)
    up = pltpu.roll(main, tm - 1, 0)
    up = jnp.concatenate([up[:tm - 8], jnp.where(rows == 7, after, up[tm - 8:])], axis=0)
    return w[0:1, :] * dn + w[1:2, :] * main + w[2:3, :] * up


def _store_groups(o_ref, x):
    for g in range(o_ref.shape[0]):
        o_ref[g] = x[:, g * SCAN_W:(g + 1) * SCAN_W].astype(o_ref.dtype)


def _terms_kernel(pr, pr_p, pr_n, pk, pk_p, pk_n, pv, pv_p, pv_n, plo, plo_p, plo_n,
                  cw_r, cw_k, cw_v, cw_l, w0, wdec, a0, wiclr, wgate, k_k, k_a, r_k, ones_bd,
                  r_o, v_o, kk_o, lw_o, kd_o, b_o, g_o, bonus_o):
    i = pl.program_id(0)
    first = i == 0
    last = i == pl.num_programs(0) - 1
    r = _conv3(pr[...], pr_p[...], pr_n[...], cw_r[...], first, last)
    k = _conv3(pk[...], pk_p[...], pk_n[...], cw_k[...], first, last)
    v = _conv3(pv[...], pv_p[...], pv_n[...], cw_v[...], first, last)
    lora = _conv3(plo[...], plo_p[...], plo_n[...], cw_l[...], first, last)
    ones = ones_bd[...]
    _store_groups(r_o, r)
    _store_groups(v_o, v)
    kk = k * k_k[...]
    kk = kk * lax.rsqrt(jnp.maximum(_segsum(kk * kk, ones), 1e-24))
    _store_groups(kk_o, kk)
    gate = _sigmoid(lora[:, 4 * LORA_PAD:4 * LORA_PAD + GATE_LORA]).astype(BF16)
    g_o[...] = _dot(gate, wgate[...]).astype(g_o.dtype)
    ksum = None
    for d in range(2):
        dec = jnp.tanh(lora[:, d * LORA_PAD:(d + 1) * LORA_PAD]).astype(BF16)
        w_pre = w0[d:d + 1, :] + _dot(dec, wdec[d])
        _store_groups(lw_o.at[d], -EXP_M05 * _sigmoid(w_pre))
        icl = lora[:, (2 + d) * LORA_PAD:(3 + d) * LORA_PAD].astype(BF16)
        a = _sigmoid(a0[d:d + 1, :] + _dot(icl, wiclr[d]))
        kd = k * (1.0 + (a - 1.0) * k_a[...])
        _store_groups(kd_o.at[d], kd)
        _store_groups(b_o.at[d], kk * a)
        ksum = kd if ksum is None else ksum + kd
    bonus_o[...] = (_segsum(r * ksum * r_k[...], ones) * v).astype(bonus_o.dtype)


def _terms_call(p, lora_block, cw_r, cw_k, cw_v, cw_l, w0, wdec, a0, wiclr, wgate, k_k, k_a, r_k, ones_bd, tm):
    t = p.shape[0]
    d = k_k.shape[1]
    nhb = t // HALO
    rh = tm // HALO

    def trio(width, cb):
        return [pl.BlockSpec((tm, width), lambda i: (i, cb)),
                pl.BlockSpec((HALO, width), lambda i: (jnp.maximum(i * rh - 1, 0), cb)),
                pl.BlockSpec((HALO, width), lambda i: (jnp.minimum((i + 1) * rh, nhb - 1), cb))]

    def full(a):
        nd = a.ndim
        return pl.BlockSpec(a.shape, lambda i: (0,) * nd)

    consts = [cw_r, cw_k, cw_v, cw_l, w0, wdec, a0, wiclr, wgate, k_k, k_a, r_k, ones_bd]
    ng = d // SCAN_W
    tok = pl.BlockSpec((tm, d), lambda i: (i, 0))
    grp = pl.BlockSpec((ng, tm, SCAN_W), lambda i: (0, i, 0))
    grp2 = pl.BlockSpec((2, ng, tm, SCAN_W), lambda i: (0, 0, i, 0))
    sd = jax.ShapeDtypeStruct((t, d), BF16)
    gsd = jax.ShapeDtypeStruct((ng, t, SCAN_W), BF16)
    gsd2 = jax.ShapeDtypeStruct((2, ng, t, SCAN_W), BF16)
    lw_sd = jax.ShapeDtypeStruct((2, ng, t, SCAN_W), F32)
    return pl.pallas_call(
        _terms_kernel,
        grid=(t // tm,),
        in_specs=trio(d, 0) + trio(d, 1) + trio(d, 2) + trio(LORA_COLS, lora_block) + [full(a) for a in consts],
        out_specs=[grp, grp, grp, grp2, grp2, grp2, tok, tok],
        out_shape=[gsd, gsd, gsd, lw_sd, gsd2, gsd2, sd, sd],
        compiler_params=_cparams(("parallel",)),
        name="terms",
    )(*([p] * 12), *consts)


def _scan_kernel(r_ref, v_ref, kk_ref, lw_ref, kd_ref, b_ref, p0_ref, y_ref, pf_ref,
                 state_ref, rm_s, yl_s, c_s, *, reverse, nck):
    L, W, G = SCAN_L, SCAN_W, SCAN_G
    i = pl.program_id(1)
    nb = pl.num_programs(1) - 1
    slot_w = i % 2
    slot_r = 1 - slot_w

    @pl.when(i == 0)
    def _():
        state_ref[...] = p0_ref[...]
        rm_s[1] = jnp.zeros(rm_s.shape[1:], rm_s.dtype)
        yl_s[1] = jnp.zeros(yl_s.shape[1:], yl_s.dtype)
        c_s[1] = jnp.zeros(c_s.shape[1:], c_s.dtype)

    row = lax.broadcasted_iota(jnp.int32, (W, W), 0)
    col = lax.broadcasted_iota(jnp.int32, (W, W), 1)
    bd = (row >> 6) == (col >> 6)
    eye_w = row == col
    tr = lax.broadcasted_iota(jnp.int32, (L, W), 0)
    tc = lax.broadcasted_iota(jnp.int32, (L, W), 1) & (L - 1)
    t1 = lax.broadcasted_iota(jnp.int32, (L, L), 0)
    t2 = lax.broadcasted_iota(jnp.int32, (L, L), 1)
    if reverse:
        strict, incl, tinc = tc > tr, tc >= tr, t2 >= t1
    else:
        strict, incl, tinc = tc < tr, tc <= tr, t2 <= t1
    tinc = jnp.where(tinc, 1.0, 0.0).astype(BF16)
    same16 = (tr >> 4) == (tc >> 4)
    same32 = (tr >> 5) == (tc >> 5)
    eye_row = jnp.where(tr == tc, 1.0, 0.0)

    def stack(x):
        return jnp.where(bd, jnp.concatenate([x] * G, axis=0), 0.0).astype(BF16)

    def mm(a, b_stacked):
        return _dot(a.astype(BF16), b_stacked)

    def prep(c):
        sl = pl.ds(c * L, L)
        lw = lw_ref[sl, :]
        h, m, l = _split3(lw)
        cs = _dot(tinc, h) + _dot(tinc, m) + _dot(tinc, l)
        yield
        r, v, kk = r_ref[sl, :].astype(F32), v_ref[sl, :].astype(F32), kk_ref[sl, :].astype(F32)
        kd, b = kd_ref[sl, :].astype(F32), b_ref[sl, :].astype(F32)
        tot = jnp.sum(lw, axis=0, keepdims=True)
        kap = kk * jnp.exp(cs - lw)
        rh = r * jnp.exp(cs)
        ginv = jnp.exp(-cs)
        gend = jnp.exp(tot - cs)
        s_bh = stack(b * ginv)
        s_kh = stack(kd * ginv)
        kr16 = jnp.concatenate([kap, rh], axis=0).astype(BF16)
        a_b = _dot_nt(kr16, s_bh)
        a_k = _dot_nt(kr16, s_kh)
        yield
        akb = jnp.where(strict, a_b[0:L], 0.0)
        arb = jnp.where(incl, a_b[L:2 * L], 0.0)
        akk = jnp.where(strict, a_k[0:L], 0.0)
        ark = jnp.where(incl, a_k[L:2 * L], 0.0)
        dg = jnp.where(same16, akb, 0.0)
        s_v = stack(v)
        d2 = mm(dg, stack(dg))
        av = mm(jnp.concatenate([akk, ark], axis=0), s_v)
        akkv, arkv = av[0:L], av[L:2 * L]
        yield
        tm_ = eye_row - dg
        both = mm(jnp.concatenate([d2, tm_], axis=0), stack(d2))
        d4, tm_ = both[0:L], tm_ + both[L:2 * L]
        yield
        both = mm(jnp.concatenate([d4, tm_], axis=0), stack(d4))
        d8, tm_ = both[0:L], tm_ + both[L:2 * L]
        yield
        tm_ = tm_ + mm(tm_, stack(d8))
        yield
        o1 = jnp.where(same16, 0.0, jnp.where(same32, akb, 0.0))
        x1 = mm(tm_, stack(o1))
        yield
        tm_ = tm_ - mm(x1, stack(tm_))
        yield
        o2 = jnp.where(same32, 0.0, akb)
        x2 = mm(tm_, stack(o2))
        yield
        tm_ = tm_ - mm(x2, stack(tm_))
        yield
        kt = mm(tm_, stack(kap))
        u = mm(tm_, stack(akkv))
        yield
        rt = rh - mm(arb, stack(kt))
        yl = arkv - mm(arb, stack(u))
        bt16 = (b * gend).astype(BF16)
        kb16 = jnp.concatenate([(kd * gend).astype(BF16), bt16], axis=0)
        vu16 = jnp.concatenate([v, -u], axis=0).astype(BF16)
        m_raw = _dot_tn(bt16, kt.astype(BF16))
        c_raw = _dot_tn(kb16, vu16)
        yield
        m_bd = jnp.where(eye_w, jnp.broadcast_to(jnp.exp(tot), (W, W)), 0.0) - jnp.where(bd, m_raw, 0.0)
        rm_s[slot_w, c, 0:L] = rt.astype(BF16)
        rm_s[slot_w, c, L:L + W] = m_bd.astype(BF16)
        yl_s[slot_w, c] = yl
        c_s[slot_w, c] = jnp.where(bd, c_raw, 0.0)

    def state_pass():
        p = state_ref[...]
        for c in (range(nck - 1, -1, -1) if reverse else range(nck)):
            rp = _dot(rm_s[slot_r, c], p.astype(BF16))
            y_ref[pl.ds(c * L, L), :] = (rp[0:L] + yl_s[slot_r, c]).astype(y_ref.dtype)
            p = jnp.where(i > 0, rp[L:L + W] + c_s[slot_r, c], p)
            yield
        state_ref[...] = p
        pf_ref[...] = p

    @pl.when(i < nb)
    def _():
        _run_interleaved([state_pass()] + [prep(c) for c in range(nck)])

    @pl.when(i == nb)
    def _():
        _run_interleaved([state_pass()])


def _scan_call(r, v, kk, lw, kd, b, p0, d, tb):
    ng, t, _ = r.shape
    nb = t // tb
    reverse = d == 1
    nck = tb // SCAN_L

    def block(step):
        step = jnp.clip(step, 0, nb - 1)
        return nb - 1 - step if reverse else step

    shared = pl.BlockSpec((None, tb, SCAN_W), lambda g, i: (g, block(i), 0))
    per_dir = pl.BlockSpec((None, None, tb, SCAN_W), lambda g, i: (d, g, block(i), 0))
    lagged = pl.BlockSpec((None, tb, SCAN_W), lambda g, i: (g, block(i - 1), 0))
    st = pl.BlockSpec((None, SCAN_W, SCAN_W), lambda g, i: (g, 0, 0))
    return pl.pallas_call(
        functools.partial(_scan_kernel, reverse=reverse, nck=nck),
        grid=(ng, nb + 1),
        in_specs=[shared, shared, shared, per_dir, per_dir, per_dir, st],
        out_specs=[lagged, st],
        out_shape=[jax.ShapeDtypeStruct((ng, t, SCAN_W), BF16), jax.ShapeDtypeStruct((ng, SCAN_W, SCAN_W), F32)],
        scratch_shapes=[pltpu.VMEM((SCAN_W, SCAN_W), F32),
                        pltpu.VMEM((2, nck, SCAN_L + SCAN_W, SCAN_W), BF16),
                        pltpu.VMEM((2, nck, SCAN_L, SCAN_W), F32),
                        pltpu.VMEM((2, nck, SCAN_W, SCAN_W), F32)],
        compiler_params=_cparams(("parallel", "arbitrary")),
        name="scan_bwd" if reverse else "scan_fwd",
    )(r, v, kk, lw, kd, b, p0)


def _gelu(x):
    return 0.5 * x * (1.0 + lax.erf(x * 0.7071067811865476))


def _merge_kernel(y0, y1, bonus, g, uv, ga, gb, lnx_g, lnx_b, lnv_g, lnv_b, ws, bsp, wpa, wpb, ones_bd, m_o):
    ones = ones_bd[...]
    inv_n = 1.0 / HEAD
    y = jnp.concatenate([y0[g_].astype(F32) + y1[g_].astype(F32) for g_ in range(y0.shape[0])], axis=1)
    mu = _segsum(y, ones) * inv_n
    yc = y - mu
    var = _segsum(yc * yc, ones) * inv_n
    yn = yc * lax.rsqrt(var + LNX_EPS) * lnx_g[...] + lnx_b[...]
    y_rwkv = ((yn + bonus[...].astype(F32)) * g[...].astype(F32)).astype(BF16)
    branch_a = _dot(y_rwkv, wpa[...])

    z = _gelu(uv[...].astype(F32))
    half = z.shape[1] // 2
    u, vv = z[:, :half], z[:, half:]
    mu = jnp.mean(vv, axis=-1, keepdims=True)
    vc = vv - mu
    var = jnp.mean(vc * vc, axis=-1, keepdims=True)
    vn = (vc * lax.rsqrt(var + LN_EPS) * lnv_g[...] + lnv_b[...]).astype(BF16)
    gch = half // GMLP_GROUPS
    rows = []
    for ch in range(z.shape[0] // CHUNK):
        cols = [_dot(ws[gi], vn[ch * CHUNK:(ch + 1) * CHUNK, gi * gch:(gi + 1) * gch]) for gi in range(GMLP_GROUPS)]
        rows.append(jnp.concatenate(cols, axis=1) + bsp[...])
    s = rows[0] if len(rows) == 1 else jnp.concatenate(rows, axis=0)
    y_gmlp = (u * s).astype(BF16)
    branch_b = _dot(y_gmlp, wpb[...])
    m_o[...] = (_sigmoid(ga[...].astype(F32)) * branch_a + _sigmoid(gb[...].astype(F32)) * branch_b).astype(BF16)


def _merge_call(y0, y1, bonus, g, p, lnx_g, lnx_b, lnv_g, lnv_b, ws, bsp, wpa, wpb, ones_bd, tm):
    t, d = bonus.shape

    def full(a):
        nd = a.ndim
        return pl.BlockSpec(a.shape, lambda i: (0,) * nd, pipeline_mode=pl.Buffered(1))

    tok = pl.BlockSpec((tm, d), lambda i: (i, 0))
    grp = pl.BlockSpec((d // SCAN_W, tm, SCAN_W), lambda i: (0, i, 0))
    consts = [lnx_g, lnx_b, lnv_g, lnv_b, ws, bsp, wpa, wpb, ones_bd]
    return pl.pallas_call(
        _merge_kernel,
        grid=(t // tm,),
        in_specs=[grp, grp, tok, tok,
                  pl.BlockSpec((tm, d), lambda i: (i, 3)),
                  pl.BlockSpec((tm, d), lambda i: (i, 4)),
                  pl.BlockSpec((tm, d), lambda i: (i, 5)),
                  ] + [full(a) for a in consts],
        out_specs=tok,
        out_shape=jax.ShapeDtypeStruct((t, d), BF16),
        compiler_params=_cparams(("parallel",)),
        name="merge",
    )(y0, y1, bonus, g, p, p, p, *consts)


def _outproj_kernel(m, w, x, gt1, g_post1, g_pre2, sc2, sh2, x1_o, h2_o, w16):
    @pl.when(pl.program_id(0) == 0)
    def _():
        w16[...] = w[...].astype(BF16)

    mix = _dot(m[...], w16[...])
    ms = jnp.mean(mix * mix, axis=-1, keepdims=True)
    x1 = x[...] + gt1[...] * (mix * lax.rsqrt(ms + RMS_EPS) * g_post1[...])
    x1_o[...] = x1
    h2_o[...] = _modulated_rmsnorm(x1, g_pre2[...], sc2[...], sh2[...]).astype(BF16)


def _outproj_call(m, w, x, gt1, g_post1, g_pre2, sc2, sh2, tm):
    t, d = x.shape
    tok = pl.BlockSpec((tm, d), lambda i: (i, 0))
    vec = pl.BlockSpec((1, d), lambda i: (0, 0))
    return pl.pallas_call(
        _outproj_kernel,
        grid=(t // tm,),
        in_specs=[tok, pl.BlockSpec((d, d), lambda i: (0, 0), pipeline_mode=pl.Buffered(1)),
                  tok, vec, vec, vec, vec, vec],
        out_specs=[tok, tok],
        out_shape=[jax.ShapeDtypeStruct((t, d), F32), jax.ShapeDtypeStruct((t, d), BF16)],
        scratch_shapes=[pltpu.VMEM((d, d), BF16)],
        compiler_params=_cparams(("arbitrary",)),
        name="outproj",
    )(m, w, x, gt1, g_post1, g_pre2, sc2, sh2)


FFN_CONV_ROWS = 128
FFN_CONV_W = 128
FFN_DOWN_ROWS = 512


def _matmul_kernel(x_ref, w_ref, o_ref, w16):
    @pl.when(pl.program_id(1) == 0)
    def _():
        w16[...] = w_ref[...].astype(BF16)

    o_ref[...] = _dot(x_ref[...], w16[...]).astype(o_ref.dtype)


def _matmul_call(x, w, tm, tn, out_dtype, name):
    m, k = x.shape
    n = w.shape[1]
    return pl.pallas_call(
        _matmul_kernel,
        grid=(n // tn, m // tm),
        in_specs=[pl.BlockSpec((tm, k), lambda j, i: (i, 0)), pl.BlockSpec((k, tn), lambda j, i: (0, j))],
        out_specs=pl.BlockSpec((tm, tn), lambda j, i: (i, j)),
        out_shape=jax.ShapeDtypeStruct((m, n), out_dtype),
        scratch_shapes=[pltpu.VMEM((k, tn), BF16)],
        compiler_params=_cparams(("parallel", "arbitrary")),
        name=name,
    )(x, w)


def _ffn_tail_kernel(ua_m, ua_p, ua_n, ub_m, ub_p, ub_n, cwa, cwb, wd, x1, gt2, g_post2, o_ref, abuf, bbuf, acc):
    i = pl.program_id(0)
    c = pl.program_id(1)
    tm = ua_m.shape[0]
    fc = ua_m.shape[1]
    first = i == 0
    last = i == pl.num_programs(0) - 1

    @pl.when(c == 0)
    def _():
        acc[...] = jnp.zeros_like(acc)

    def fill(buf, main, prev, nxt):
        zero = jnp.zeros(prev.shape, prev.dtype)
        buf[0:GRID_W, :] = jnp.where(first, zero, prev[...])
        buf[GRID_W:GRID_W + tm, :] = main[...]
        buf[GRID_W + tm:GRID_W + tm + GRID_W, :] = jnp.where(last, zero, nxt[...])

    fill(abuf, ua_m, ua_p, ua_n)
    fill(bbuf, ub_m, ub_p, ub_n)

    nr = FFN_CONV_ROWS
    rt_ = lax.broadcasted_iota(jnp.int32, (nr, 3 * nr), 0)
    ct_ = lax.broadcasted_iota(jnp.int32, (nr, 3 * nr), 1)
    gcol = rt_ & (GRID_W - 1)
    hit = ((ct_ == rt_ - 1) & (gcol != 0)) | (ct_ == rt_ + nr) | ((ct_ == rt_ + 2 * nr + 1) & (gcol != GRID_W - 1))
    shift_add = jnp.where(hit, 1.0, 0.0).astype(BF16)

    def conv_rows(u_ref, cw_ref, r0):
        cw = cw_ref[...].astype(BF16)
        taps = [u_ref[r0 + dr * GRID_W:r0 + dr * GRID_W + nr, :] for dr in range(3)]
        col_sums = [cw[dc:dc + 1, :] * taps[0] + cw[3 + dc:4 + dc, :] * taps[1] + cw[6 + dc:7 + dc, :] * taps[2]
                    for dc in range(3)]
        return _dot(shift_add, jnp.concatenate(col_sums, axis=0))

    for m0 in range(0, tm, FFN_DOWN_ROWS):
        row_blocks = []
        for r0 in range(m0, m0 + FFN_DOWN_ROWS, nr):
            a = conv_rows(abuf, cwa, r0)
            b = conv_rows(bbuf, cwb, r0)
            row_blocks.append((a * _sigmoid(a) * b).astype(BF16))
        acc[m0:m0 + FFN_DOWN_ROWS, :] += _dot(jnp.concatenate(row_blocks, axis=0), wd[...])

    @pl.when(c == pl.num_programs(1) - 1)
    def _():
        f = acc[...]
        ms = jnp.mean(f * f, axis=-1, keepdims=True)
        o_ref[...] = x1[...] + gt2[...] * (f * lax.rsqrt(ms + RMS_EPS) * g_post2[...])


def _ffn_tail_call(up, cw, w_down, x1, gt2, g_post2, tm, fc):
    t, d = x1.shape
    dff = w_down.shape[0]
    nc = dff // fc
    hb = tm // GRID_W
    nhb = t // GRID_W

    def trio(off):
        return [pl.BlockSpec((tm, fc), lambda i, c: (i, off + c)),
                pl.BlockSpec((GRID_W, fc), lambda i, c: (jnp.maximum(i * hb - 1, 0), off + c)),
                pl.BlockSpec((GRID_W, fc), lambda i, c: (jnp.minimum((i + 1) * hb, nhb - 1), off + c))]

    tok = pl.BlockSpec((tm, d), lambda i, c: (i, 0))
    vec = pl.BlockSpec((1, d), lambda i, c: (0, 0))
    rows_all = tm + 2 * GRID_W
    return pl.pallas_call(
        _ffn_tail_kernel,
        grid=(t // tm, nc),
        in_specs=trio(0) + trio(nc) + [pl.BlockSpec((9, fc), lambda i, c: (0, c)),
                                       pl.BlockSpec((9, fc), lambda i, c: (0, nc + c)),
                                       pl.BlockSpec((fc, d), lambda i, c: (c, 0)),
                                       tok, vec, vec],
        out_specs=tok,
        out_shape=jax.ShapeDtypeStruct((t, d), F32),
        scratch_shapes=[pltpu.VMEM((rows_all, fc), BF16), pltpu.VMEM((rows_all, fc), BF16), pltpu.VMEM((tm, d), F32)],
        compiler_params=_cparams(("parallel", "arbitrary")),
        name="ffn_tail",
    )(*([up] * 6), cw, cw, w_down, x1, gt2, g_post2)


def _pad_lora(a, axis):
    pad = [(0, 0)] * a.ndim
    pad[axis] = (0, LORA_PAD - a.shape[axis])
    return jnp.pad(a, pad)


def kernel(x, c, ctx, c_ctx, w_mod, b_mod, g_pre1, g_post1, g_pre2, g_post2, w_in, conv_rwkv, w0, w_decay_up, a0, w_iclr_up, w_gate_up, k_k, k_a, r_k, lnx_g, lnx_b, ln_v_g, ln_v_b, w_spatial, b_spatial, w_proj_a, w_proj_b, w_out, w_up, conv_ffn, w_down):
    assert x.shape[0] == 1 and w_mod.shape[0] == 1
    xt, ct = x[0], ctx[0]
    t, d = xt.shape
    tc = ct.shape[0]
    row = lambda a: a.reshape(1, -1)

    off_dec = 3 * d
    off_iclr = off_dec + 2 * DECAY_LORA
    off_gate = off_iclr + 2 * ICLR_LORA
    rwkv_cols = off_gate + GATE_LORA

    def lora_layout(a):
        parts = [_pad_lora(a[:, off_dec + k * DECAY_LORA:off_dec + (k + 1) * DECAY_LORA], 1) for k in range(2)]
        parts += [_pad_lora(a[:, off_iclr + k * ICLR_LORA:off_iclr + (k + 1) * ICLR_LORA], 1) for k in range(2)]
        parts += [a[:, off_gate:rwkv_cols], jnp.zeros((a.shape[0], LORA_COLS - 4 * LORA_PAD - GATE_LORA), a.dtype)]
        return jnp.concatenate(parts, axis=1)

    wi = w_in[0]
    w_in_p = jnp.concatenate([wi[:, :off_dec].astype(BF16), wi[:, rwkv_cols:].astype(BF16),
                              lora_layout(wi).astype(BF16)], axis=1)
    cr = conv_rwkv[0]
    cw_r, cw_k, cw_v, cw_l = cr[:, :d], cr[:, d:2 * d], cr[:, 2 * d:3 * d], lora_layout(cr)
    wdec = _pad_lora(w_decay_up[0], 1).astype(BF16)
    wiclr = _pad_lora(w_iclr_up[0], 1).astype(BF16)
    wgate = w_gate_up[0].astype(BF16)
    ones_bd = jnp.kron(jnp.eye(SCAN_G, dtype=F32), jnp.ones((HEAD, HEAD), F32)).astype(BF16)
    gch = ln_v_g.shape[1] // GMLP_GROUPS
    bsp = jnp.repeat(b_spatial[0].T, gch, axis=1)
    cw_ffn = conv_ffn[0].reshape(9, -1)

    mod = _mod_call(jnp.stack([c[0], c_ctx], axis=1), w_mod[0], row(b_mod[0]))
    sh1, sc1, gt1, sh2, sc2, gt2 = [mod[0:1, k * d:(k + 1) * d] for k in range(6)]
    csh1, csc1 = mod[1:2, 0:d], mod[1:2, d:2 * d]

    n_blocks = w_in_p.shape[1] // 1024
    p = _inproj_call(xt, row(g_pre1[0]), sc1, sh1, w_in_p, list(range(n_blocks)), min(1024, t), 1024)
    pc = _inproj_call(ct, row(g_pre1[0]), csc1, csh1, w_in_p, list(range(6)) + [n_blocks - 1], min(256, tc), 1024)

    targs = (cw_r, cw_k, cw_v, cw_l, w0[0], wdec, a0[0], wiclr, wgate, row(k_k[0]), row(k_a[0]), row(r_k[0]), ones_bd)
    r_c, v_c, kk_c, lw_c, kd_c, b_c, _, _ = _terms_call(pc, 6, *targs, tm=min(256, tc))
    r_x, v_x, kk_x, lw_x, kd_x, b_x, g_x, bonus_x = _terms_call(p, n_blocks - 1, *targs, tm=256)
    zero_state = jnp.zeros((d // SCAN_W, SCAN_W, SCAN_W), F32)
    ys = []
    for dirn in range(2):
        _, s_ctx = _scan_call(r_c, v_c, kk_c, lw_c, kd_c, b_c, zero_state, dirn, min(256, tc))
        y_d, _ = _scan_call(r_x, v_x, kk_x, lw_x, kd_x, b_x, s_ctx, dirn, min(512, t))
        ys.append(y_d)

    m = _merge_call(ys[0], ys[1], bonus_x, g_x, p, row(lnx_g[0]), row(lnx_b[0]), row(ln_v_g[0]), row(ln_v_b[0]),
                    w_spatial[0].astype(BF16), bsp, w_proj_a[0].astype(BF16), w_proj_b[0].astype(BF16), ones_bd,
                    tm=min(256, t))
    x1, h2 = _outproj_call(m, w_out[0], xt, gt1, row(g_post1[0]), row(g_pre2[0]), sc2, sh2,
                           tm=min(512, t))

    up = _matmul_call(h2, w_up[0], min(2048, t), 1024, BF16, "ffn_up")
    out = _ffn_tail_call(up, cw_ffn, w_down[0].astype(BF16), x1, gt2, row(g_post2[0]), tm=min(512, t), fc=512)
    return out[None]
```

```python
import functools

import jax
import jax.numpy as jnp
from jax import lax
from jax.experimental import pallas as pl
from jax.experimental.pallas import tpu as pltpu

F32 = jnp.float32
BF16 = jnp.bfloat16

RMS_EPS = 1e-6
LNX_EPS = 64e-5
LN_EPS = 1e-5
HEAD = 64
GRID_W = 64
CHUNK = 128
GMLP_GROUPS = 8
DECAY_LORA = 96
ICLR_LORA = 96
GATE_LORA = 256
LORA_PAD = 128
LORA_COLS = 1024
EXP_M05 = 0.6065306597126334

SCAN_L = 64
SCAN_G = 4
SCAN_W = SCAN_G * HEAD

VMEM_LIMIT = 56 * 1024 * 1024


def _cparams(sem):
    return pltpu.CompilerParams(dimension_semantics=sem, vmem_limit_bytes=VMEM_LIMIT)


def _dot(a, b):
    return jnp.dot(a, b, preferred_element_type=F32)


def _dot_nt(a, b):
    return lax.dot_general(a, b, (((1,), (1,)), ((), ())), preferred_element_type=F32)


def _dot_tn(a, b):
    return lax.dot_general(a, b, (((0,), (0,)), ((), ())), preferred_element_type=F32)


def _sigmoid(x):
    return 1.0 / (1.0 + jnp.exp(-x))


def _split3(x):
    h = x.astype(BF16)
    r1 = x - h.astype(F32)
    m = r1.astype(BF16)
    l = (r1 - m.astype(F32)).astype(BF16)
    return h, m, l


def _segsum(x, ones_bd):
    rows, width = x.shape
    outs = []
    for g in range(width // SCAN_W):
        xg = x[:, g * SCAN_W:(g + 1) * SCAN_W]
        h = xg.astype(BF16)
        l = (xg - h.astype(F32)).astype(BF16)
        s = _dot(jnp.concatenate([h, l], axis=0), ones_bd)
        outs.append(s[0:rows] + s[rows:2 * rows])
    return outs[0] if len(outs) == 1 else jnp.concatenate(outs, axis=1)


def _run_interleaved(gens):
    while gens:
        still = []
        for gen in gens:
            try:
                next(gen)
                still.append(gen)
            except StopIteration:
                pass
        gens = still


def _mod_kernel(cc_ref, w_ref, b_ref, o_ref):
    s = cc_ref[...]
    s = s * _sigmoid(s)
    w = w_ref[...]
    r0 = jnp.sum(s[:, 0:1] * w, axis=0, keepdims=True)
    r1 = jnp.sum(s[:, 1:2] * w, axis=0, keepdims=True)
    o_ref[...] = jnp.concatenate([r0, r1], axis=0) + b_ref[...]


def _mod_call(cc, w_mod, b_mod):
    d, n = w_mod.shape
    tn = 1024
    return pl.pallas_call(
        _mod_kernel,
        grid=(n // tn,),
        in_specs=[pl.BlockSpec((d, 2), lambda j: (0, 0)),
                  pl.BlockSpec((d, tn), lambda j: (0, j)),
                  pl.BlockSpec((1, tn), lambda j: (0, j))],
        out_specs=pl.BlockSpec((2, tn), lambda j: (0, j)),
        out_shape=jax.ShapeDtypeStruct((2, n), F32),
        compiler_params=_cparams(("parallel",)),
        name="mod",
    )(cc, w_mod, b_mod)


def _modulated_rmsnorm(x, g, sc, sh):
    ms = jnp.mean(x * x, axis=-1, keepdims=True)
    return (x * lax.rsqrt(ms + RMS_EPS) * g) * (1.0 + sc) + sh


def _inproj_kernel(x_ref, g_ref, sc_ref, sh_ref, w_ref, wl_ref, o_ref, h_ref, *, n_main):
    j = pl.program_id(1)

    @pl.when(j == 0)
    def _():
        h_ref[...] = _modulated_rmsnorm(x_ref[...], g_ref[...], sc_ref[...], sh_ref[...]).astype(BF16)

    @pl.when(j < n_main)
    def _():
        o_ref[...] = _dot(h_ref[...], w_ref[...]).astype(o_ref.dtype)

    @pl.when(j == n_main)
    def _():
        o_ref[...] = _dot(h_ref[...], wl_ref[...]).astype(o_ref.dtype)


def _inproj_call(x, g, sc, sh, w, w_lora, n_first, skip, n_main, tm, tn):
    t, d = x.shape

    lane = 128
    assert tn % lane == 0 and skip % lane == 0

    def wstart(i, j):
        jj = jnp.minimum(j, n_main - 1)
        return 0, (jj * (tn // lane) + jnp.where(jj >= n_first, skip // lane, 0)) * lane

    vec = pl.BlockSpec((1, d), lambda i, j: (0, 0))
    return pl.pallas_call(
        functools.partial(_inproj_kernel, n_main=n_main),
        grid=(t // tm, n_main + 1),
        in_specs=[pl.BlockSpec((tm, d), lambda i, j: (i, 0)), vec, vec, vec,
                  pl.BlockSpec((pl.Element(d), pl.Element(tn)), wstart),
                  pl.BlockSpec((d, tn), lambda i, j: (0, 0))],
        out_specs=pl.BlockSpec((tm, tn), lambda i, j: (i, j)),
        out_shape=jax.ShapeDtypeStruct((t, (n_main + 1) * tn), BF16),
        scratch_shapes=[pltpu.VMEM((tm, d), BF16)],
        compiler_params=_cparams(("parallel", "arbitrary")),
        name="inproj",
    )(x, g, sc, sh, w, w_lora)


HALO = 16


def _conv3(main, prev_blk, next_blk, w, first, last):
    main = main.astype(F32)
    tm = main.shape[0]
    rows = lax.broadcasted_iota(jnp.int32, (8, main.shape[1]), 0)
    before = jnp.where(first, 0.0, prev_blk[HALO - 1:HALO, :].astype(F32))
    after = jnp.where(last, 0.0, next_blk[0:1, :].astype(F32))
    dn = pltpu.roll(main, 1, 0)
    dn = jnp.concatenate([jnp.where(rows == 0, before, dn[0:8]), dn[8:]], axis=0)
    up = pltpu.roll(main, tm - 1, 0)
    up = jnp.concatenate([up[:tm - 8], jnp.where(rows == 7, after, up[tm - 8:])], axis=0)
    return w[0:1, :] * dn + w[1:2, :] * main + w[2:3, :] * up


def _store_groups(o_ref, x):
    for g in range(o_ref.shape[0]):
        o_ref[g] = x[:, g * SCAN_W:(g + 1) * SCAN_W].astype(o_ref.dtype)


def _terms_kernel(pr, pr_p, pr_n, pk, pk_p, pk_n, pv, pv_p, pv_n, plo, plo_p, plo_n,
                  cw_r, cw_k, cw_v, cw_l, w0, wdec, a0, wiclr, wgate, k_k, k_a, r_k, ones_bd,
                  r_o, v_o, kk_o, lw_o, kd_o, b_o, g_o, bonus_o):
    i = pl.program_id(0)
    first = i == 0
    last = i == pl.num_programs(0) - 1
    r = _conv3(pr[...], pr_p[...], pr_n[...], cw_r[...], first, last)
    k = _conv3(pk[...], pk_p[...], pk_n[...], cw_k[...], first, last)
    v = _conv3(pv[...], pv_p[...], pv_n[...], cw_v[...], first, last)
    lora = _conv3(plo[...], plo_p[...], plo_n[...], cw_l[...], first, last)
    ones = ones_bd[...]
    _store_groups(r_o, r)
    _store_groups(v_o, v)
    kk = k * k_k[...]
    kk = kk * lax.rsqrt(jnp.maximum(_segsum(kk * kk, ones), 1e-24))
    _store_groups(kk_o, kk)
    gate = _sigmoid(lora[:, 4 * LORA_PAD:4 * LORA_PAD + GATE_LORA]).astype(BF16)
    g_o[...] = _dot(gate, wgate[...]).astype(g_o.dtype)
    ksum = None
    for d in range(2):
        dec = jnp.tanh(lora[:, d * LORA_PAD:(d + 1) * LORA_PAD]).astype(BF16)
        w_pre = w0[d:d + 1, :] + _dot(dec, wdec[d])
        _store_groups(lw_o.at[d], -EXP_M05 * _sigmoid(w_pre))
        icl = lora[:, (2 + d) * LORA_PAD:(3 + d) * LORA_PAD].astype(BF16)
        a = _sigmoid(a0[d:d + 1, :] + _dot(icl, wiclr[d]))
        kd = k * (1.0 + (a - 1.0) * k_a[...])
        _store_groups(kd_o.at[d], kd)
        _store_groups(b_o.at[d], kk * a)
        ksum = kd if ksum is None else ksum + kd
    bonus_o[...] = (_segsum(r * ksum * r_k[...], ones) * v).astype(bonus_o.dtype)


def _terms_call(p, lora_block, cw_r, cw_k, cw_v, cw_l, w0, wdec, a0, wiclr, wgate, k_k, k_a, r_k, ones_bd, tm):
    t = p.shape[0]
    d = k_k.shape[1]
    nhb = t // HALO
    rh = tm // HALO

    def trio(width, cb):
        return [pl.BlockSpec((tm, width), lambda i: (i, cb)),
                pl.BlockSpec((HALO, width), lambda i: (jnp.maximum(i * rh - 1, 0), cb)),
                pl.BlockSpec((HALO, width), lambda i: (jnp.minimum((i + 1) * rh, nhb - 1), cb))]

    def full(a):
        nd = a.ndim
        return pl.BlockSpec(a.shape, lambda i: (0,) * nd)

    consts = [cw_r, cw_k, cw_v, cw_l, w0, wdec, a0, wiclr, wgate, k_k, k_a, r_k, ones_bd]
    ng = d // SCAN_W
    tok = pl.BlockSpec((tm, d), lambda i: (i, 0))
    grp = pl.BlockSpec((ng, tm, SCAN_W), lambda i: (0, i, 0))
    grp2 = pl.BlockSpec((2, ng, tm, SCAN_W), lambda i: (0, 0, i, 0))
    sd = jax.ShapeDtypeStruct((t, d), BF16)
    gsd = jax.ShapeDtypeStruct((ng, t, SCAN_W), BF16)
    gsd2 = jax.ShapeDtypeStruct((2, ng, t, SCAN_W), BF16)
    lw_sd = jax.ShapeDtypeStruct((2, ng, t, SCAN_W), F32)
    return pl.pallas_call(
        _terms_kernel,
        grid=(t // tm,),
        in_specs=trio(d, 0) + trio(d, 1) + trio(d, 2) + trio(LORA_COLS, lora_block) + [full(a) for a in consts],
        out_specs=[grp, grp, grp, grp2, grp2, grp2, tok, tok],
        out_shape=[gsd, gsd, gsd, lw_sd, gsd2, gsd2, sd, sd],
        compiler_params=_cparams(("parallel",)),
        name="terms",
    )(*([p] * 12), *consts)


def _scan_kernel(r_ref, v_ref, kk_ref, lw_ref, kd_ref, b_ref, p0_ref, y_ref, pf_ref,
                 state_ref, rm_s, yl_s, c_s, *, reverse, nck):
    L, W, G = SCAN_L, SCAN_W, SCAN_G
    i = pl.program_id(1)
    nb = pl.num_programs(1) - 1
    slot_w = i % 2
    slot_r = 1 - slot_w

    @pl.when(i == 0)
    def _():
        state_ref[...] = p0_ref[...]
        rm_s[1] = jnp.zeros(rm_s.shape[1:], rm_s.dtype)
        yl_s[1] = jnp.zeros(yl_s.shape[1:], yl_s.dtype)
        c_s[1] = jnp.zeros(c_s.shape[1:], c_s.dtype)

    row = lax.broadcasted_iota(jnp.int32, (W, W), 0)
    col = lax.broadcasted_iota(jnp.int32, (W, W), 1)
    bd = (row >> 6) == (col >> 6)
    eye_w = row == col
    tr = lax.broadcasted_iota(jnp.int32, (L, W), 0)
    tc = lax.broadcasted_iota(jnp.int32, (L, W), 1) & (L - 1)
    t1 = lax.broadcasted_iota(jnp.int32, (L, L), 0)
    t2 = lax.broadcasted_iota(jnp.int32, (L, L), 1)
    if reverse:
        strict, incl, tinc = tc > tr, tc >= tr, t2 >= t1
    else:
        strict, incl, tinc = tc < tr, tc <= tr, t2 <= t1
    tinc = jnp.where(tinc, 1.0, 0.0).astype(BF16)
    same16 = (tr >> 4) == (tc >> 4)
    same32 = (tr >> 5) == (tc >> 5)
    eye_row = jnp.where(tr == tc, 1.0, 0.0)

    def stack(x):
        return jnp.where(bd, jnp.concatenate([x] * G, axis=0), 0.0).astype(BF16)

    def mm(a, b_stacked):
        return _dot(a.astype(BF16), b_stacked)

    def prep(c):
        sl = pl.ds(c * L, L)
        lw = lw_ref[sl, :]
        h, m, l = _split3(lw)
        cs = _dot(tinc, h) + _dot(tinc, m) + _dot(tinc, l)
        yield
        r, v, kk = r_ref[sl, :].astype(F32), v_ref[sl, :].astype(F32), kk_ref[sl, :].astype(F32)
        kd, b = kd_ref[sl, :].astype(F32), b_ref[sl, :].astype(F32)
        tot = jnp.sum(lw, axis=0, keepdims=True)
        kap = kk * jnp.exp(cs - lw)
        rh = r * jnp.exp(cs)
        ginv = jnp.exp(-cs)
        gend = jnp.exp(tot - cs)
        s_bh = stack(b * ginv)
        s_kh = stack(kd * ginv)
        kr16 = jnp.concatenate([kap, rh], axis=0).astype(BF16)
        a_b = _dot_nt(kr16, s_bh)
        a_k = _dot_nt(kr16, s_kh)
        yield
        akb = jnp.where(strict, a_b[0:L], 0.0)
        arb = jnp.where(incl, a_b[L:2 * L], 0.0)
        akk = jnp.where(strict, a_k[0:L], 0.0)
        ark = jnp.where(incl, a_k[L:2 * L], 0.0)
        dg = jnp.where(same16, akb, 0.0)
        s_v = stack(v)
        d2 = mm(dg, stack(dg))
        av = mm(jnp.concatenate([akk, ark], axis=0), s_v)
        akkv, arkv = av[0:L], av[L:2 * L]
        yield
        tm_ = eye_row - dg
        both = mm(jnp.concatenate([d2, tm_], axis=0), stack(d2))
        d4, tm_ = both[0:L], tm_ + both[L:2 * L]
        yield
        both = mm(jnp.concatenate([d4, tm_], axis=0), stack(d4))
        d8, tm_ = both[0:L], tm_ + both[L:2 * L]
        yield
        tm_ = tm_ + mm(tm_, stack(d8))
        yield
        o1 = jnp.where(same16, 0.0, jnp.where(same32, akb, 0.0))
        x1 = mm(tm_, stack(o1))
        yield
        tm_ = tm_ - mm(x1, stack(tm_))
        yield
        o2 = jnp.where(same32, 0.0, akb)
        x2 = mm(tm_, stack(o2))
        yield
        tm_ = tm_ - mm(x2, stack(tm_))
        yield
        kt = mm(tm_, stack(kap))
        u = mm(tm_, stack(akkv))
        yield
        rt = rh - mm(arb, stack(kt))
        yl = arkv - mm(arb, stack(u))
        bt16 = (b * gend).astype(BF16)
        kb16 = jnp.concatenate([(kd * gend).astype(BF16), bt16], axis=0)
        vu16 = jnp.concatenate([v, -u], axis=0).astype(BF16)
        m_raw = _dot_tn(bt16, kt.astype(BF16))
        c_raw = _dot_tn(kb16, vu16)
        yield
        m_bd = jnp.where(eye_w, jnp.broadcast_to(jnp.exp(tot), (W, W)), 0.0) - jnp.where(bd, m_raw, 0.0)
        rm_s[slot_w, c, 0:L] = rt.astype(BF16)
        rm_s[slot_w, c, L:L + W] = m_bd.astype(BF16)
        yl_s[slot_w, c] = yl
        c_s[slot_w, c] = jnp.where(bd, c_raw, 0.0)

    def state_pass():
        p = state_ref[...]
        for c in (range(nck - 1, -1, -1) if reverse else range(nck)):
            rp = _dot(rm_s[slot_r, c], p.astype(BF16))
            y_ref[pl.ds(c * L, L), :] = (rp[0:L] + yl_s[slot_r, c]).astype(y_ref.dtype)
            p = jnp.where(i > 0, rp[L:L + W] + c_s[slot_r, c], p)
            yield
        state_ref[...] = p
        pf_ref[...] = p

    @pl.when(i < nb)
    def _():
        _run_interleaved([state_pass()] + [prep(c) for c in range(nck)])

    @pl.when(i == nb)
    def _():
        _run_interleaved([state_pass()])


def _scan_call(r, v, kk, lw, kd, b, p0, d, tb):
    ng, t, _ = r.shape
    nb = t // tb
    reverse = d == 1
    nck = tb // SCAN_L

    def block(step):
        step = jnp.clip(step, 0, nb - 1)
        return nb - 1 - step if reverse else step

    shared = pl.BlockSpec((None, tb, SCAN_W), lambda g, i: (g, block(i), 0))
    per_dir = pl.BlockSpec((None, None, tb, SCAN_W), lambda g, i: (d, g, block(i), 0))
    lagged = pl.BlockSpec((None, tb, SCAN_W), lambda g, i: (g, block(i - 1), 0))
    st = pl.BlockSpec((None, SCAN_W, SCAN_W), lambda g, i: (g, 0, 0))
    return pl.pallas_call(
        functools.partial(_scan_kernel, reverse=reverse, nck=nck),
        grid=(ng, nb + 1),
        in_specs=[shared, shared, shared, per_dir, per_dir, per_dir, st],
        out_specs=[lagged, st],
        out_shape=[jax.ShapeDtypeStruct((ng, t, SCAN_W), BF16), jax.ShapeDtypeStruct((ng, SCAN_W, SCAN_W), F32)],
        scratch_shapes=[pltpu.VMEM((SCAN_W, SCAN_W), F32),
                        pltpu.VMEM((2, nck, SCAN_L + SCAN_W, SCAN_W), BF16),
                        pltpu.VMEM((2, nck, SCAN_L, SCAN_W), F32),
                        pltpu.VMEM((2, nck, SCAN_W, SCAN_W), F32)],
        compiler_params=_cparams(("parallel", "arbitrary")),
        name="scan_bwd" if reverse else "scan_fwd",
    )(r, v, kk, lw, kd, b, p0)


def _gelu(x):
    return 0.5 * x * (1.0 + lax.erf(x * 0.7071067811865476))


def _merge_kernel(y0, y1, bonus, g, uv, ga, gb, lnx_g, lnx_b, lnv_g, lnv_b, ws, bsp, wpa, wpb, ones_bd, m_o):
    ones = ones_bd[...]
    inv_n = 1.0 / HEAD
    y = jnp.concatenate([y0[g_].astype(F32) + y1[g_].astype(F32) for g_ in range(y0.shape[0])], axis=1)
    mu = _segsum(y, ones) * inv_n
    yc = y - mu
    var = _segsum(yc * yc, ones) * inv_n
    yn = yc * lax.rsqrt(var + LNX_EPS) * lnx_g[...] + lnx_b[...]
    y_rwkv = ((yn + bonus[...].astype(F32)) * g[...].astype(F32)).astype(BF16)
    branch_a = _dot(y_rwkv, wpa[...])

    z = _gelu(uv[...].astype(F32))
    half = z.shape[1] // 2
    u, vv = z[:, :half], z[:, half:]
    mu = jnp.mean(vv, axis=-1, keepdims=True)
    vc = vv - mu
    var = jnp.mean(vc * vc, axis=-1, keepdims=True)
    vn = (vc * lax.rsqrt(var + LN_EPS) * lnv_g[...] + lnv_b[...]).astype(BF16)
    gch = half // GMLP_GROUPS
    rows = []
    for ch in range(z.shape[0] // CHUNK):
        cols = [_dot(ws[gi], vn[ch * CHUNK:(ch + 1) * CHUNK, gi * gch:(gi + 1) * gch]) for gi in range(GMLP_GROUPS)]
        rows.append(jnp.concatenate(cols, axis=1) + bsp[...])
    s = rows[0] if len(rows) == 1 else jnp.concatenate(rows, axis=0)
    y_gmlp = (u * s).astype(BF16)
    branch_b = _dot(y_gmlp, wpb[...])
    m_o[...] = (_sigmoid(ga[...].astype(F32)) * branch_a + _sigmoid(gb[...].astype(F32)) * branch_b).astype(BF16)


def _merge_call(y0, y1, bonus, g, p, lnx_g, lnx_b, lnv_g, lnv_b, ws, bsp, wpa, wpb, ones_bd, tm):
    t, d = bonus.shape

    def full(a):
        nd = a.ndim
        return pl.BlockSpec(a.shape, lambda i: (0,) * nd, pipeline_mode=pl.Buffered(1))

    tok = pl.BlockSpec((tm, d), lambda i: (i, 0))
    grp = pl.BlockSpec((d // SCAN_W, tm, SCAN_W), lambda i: (0, i, 0))
    consts = [lnx_g, lnx_b, lnv_g, lnv_b, ws, bsp, wpa, wpb, ones_bd]
    return pl.pallas_call(
        _merge_kernel,
        grid=(t // tm,),
        in_specs=[grp, grp, tok, tok,
                  pl.BlockSpec((tm, d), lambda i: (i, 3)),
                  pl.BlockSpec((tm, d), lambda i: (i, 4)),
                  pl.BlockSpec((tm, d), lambda i: (i, 5)),
                  ] + [full(a) for a in consts],
        out_specs=tok,
        out_shape=jax.ShapeDtypeStruct((t, d), BF16),
        compiler_params=_cparams(("parallel",)),
        name="merge",
    )(y0, y1, bonus, g, p, p, p, *consts)


def _outproj_kernel(m, w, x, gt1, g_post1, g_pre2, sc2, sh2, x1_o, h2_o, w16):
    @pl.when(pl.program_id(0) == 0)
    def _():
        w16[...] = w[...].astype(BF16)

    mix = _dot(m[...], w16[...])
    ms = jnp.mean(mix * mix, axis=-1, keepdims=True)
    x1 = x[...] + gt1[...] * (mix * lax.rsqrt(ms + RMS_EPS) * g_post1[...])
    x1_o[...] = x1
    h2_o[...] = _modulated_rmsnorm(x1, g_pre2[...], sc2[...], sh2[...]).astype(BF16)


def _outproj_call(m, w, x, gt1, g_post1, g_pre2, sc2, sh2, tm):
    t, d = x.shape
    tok = pl.BlockSpec((tm, d), lambda i: (i, 0))
    vec = pl.BlockSpec((1, d), lambda i: (0, 0))
    return pl.pallas_call(
        _outproj_kernel,
        grid=(t // tm,),
        in_specs=[tok, pl.BlockSpec((d, d), lambda i: (0, 0), pipeline_mode=pl.Buffered(1)),
                  tok, vec, vec, vec, vec, vec],
        out_specs=[tok, tok],
        out_shape=[jax.ShapeDtypeStruct((t, d), F32), jax.ShapeDtypeStruct((t, d), BF16)],
        scratch_shapes=[pltpu.VMEM((d, d), BF16)],
        compiler_params=_cparams(("arbitrary",)),
        name="outproj",
    )(m, w, x, gt1, g_post1, g_pre2, sc2, sh2)


FFN_CONV_ROWS = 64
FFN_CONV_W = 128
FFN_DOWN_ROWS = 512


def _matmul_kernel(x_ref, w_ref, o_ref, w16):
    @pl.when(pl.program_id(1) == 0)
    def _():
        w16[...] = w_ref[...].astype(BF16)

    o_ref[...] = _dot(x_ref[...], w16[...]).astype(o_ref.dtype)


def _matmul_call(x, w, tm, tn, out_dtype, name):
    m, k = x.shape
    n = w.shape[1]
    return pl.pallas_call(
        _matmul_kernel,
        grid=(n // tn, m // tm),
        in_specs=[pl.BlockSpec((tm, k), lambda j, i: (i, 0)), pl.BlockSpec((k, tn), lambda j, i: (0, j))],
        out_specs=pl.BlockSpec((tm, tn), lambda j, i: (i, j)),
        out_shape=jax.ShapeDtypeStruct((m, n), out_dtype),
        scratch_shapes=[pltpu.VMEM((k, tn), BF16)],
        compiler_params=_cparams(("parallel", "arbitrary")),
        name=name,
    )(x, w)


def _ffn_tail_kernel(ua_m, ua_p, ua_n, ub_m, ub_p, ub_n, cwa, cwb, wd, x1, gt2, g_post2, o_ref, abuf, bbuf, acc):
    i = pl.program_id(0)
    c = pl.program_id(1)
    tm = ua_m.shape[0]
    fc = ua_m.shape[1]
    first = i == 0
    last = i == pl.num_programs(0) - 1

    @pl.when(c == 0)
    def _():
        acc[...] = jnp.zeros_like(acc)

    def fill(buf, main, prev, nxt):
        zero = jnp.zeros(prev.shape, prev.dtype)
        buf[0:GRID_W, :] = jnp.where(first, zero, prev[...])
        buf[GRID_W:GRID_W + tm, :] = main[...]
        buf[GRID_W + tm:GRID_W + tm + GRID_W, :] = jnp.where(last, zero, nxt[...])

    fill(abuf, ua_m, ua_p, ua_n)
    fill(bbuf, ub_m, ub_p, ub_n)

    nr = FFN_CONV_ROWS
    rt_ = lax.broadcasted_iota(jnp.int32, (nr, 3 * nr), 0)
    ct_ = lax.broadcasted_iota(jnp.int32, (nr, 3 * nr), 1)
    gcol = rt_ & (GRID_W - 1)
    hit = ((ct_ == rt_ - 1) & (gcol != 0)) | (ct_ == rt_ + nr) | ((ct_ == rt_ + 2 * nr + 1) & (gcol != GRID_W - 1))
    shift_add = jnp.where(hit, 1.0, 0.0).astype(BF16)

    def conv_rows(u_ref, cw_ref, r0):
        cw = cw_ref[...].astype(BF16)
        taps = [u_ref[r0 + dr * GRID_W:r0 + dr * GRID_W + nr, :] for dr in range(3)]
        col_sums = [cw[dc:dc + 1, :] * taps[0] + cw[3 + dc:4 + dc, :] * taps[1] + cw[6 + dc:7 + dc, :] * taps[2]
                    for dc in range(3)]
        return _dot(shift_add, jnp.concatenate(col_sums, axis=0))

    for m0 in range(0, tm, FFN_DOWN_ROWS):
        row_blocks = []
        for r0 in range(m0, m0 + FFN_DOWN_ROWS, nr):
            a = conv_rows(abuf, cwa, r0)
            b = conv_rows(bbuf, cwb, r0)
            row_blocks.append((a * _sigmoid(a) * b).astype(BF16))
        acc[m0:m0 + FFN_DOWN_ROWS, :] += _dot(jnp.concatenate(row_blocks, axis=0), wd[...])

    @pl.when(c == pl.num_programs(1) - 1)
    def _():
        f = acc[...]
        ms = jnp.mean(f * f, axis=-1, keepdims=True)
        o_ref[...] = x1[...] + gt2[...] * (f * lax.rsqrt(ms + RMS_EPS) * g_post2[...])


def _ffn_tail_call(up, cw, w_down, x1, gt2, g_post2, tm, fc):
    t, d = x1.shape
    dff = w_down.shape[0]
    nc = dff // fc
    hb = tm // GRID_W
    nhb = t // GRID_W

    def trio(off):
        return [pl.BlockSpec((tm, fc), lambda i, c: (i, off + c)),
                pl.BlockSpec((GRID_W, fc), lambda i, c: (jnp.maximum(i * hb - 1, 0), off + c)),
                pl.BlockSpec((GRID_W, fc), lambda i, c: (jnp.minimum((i + 1) * hb, nhb - 1), off + c))]

    tok = pl.BlockSpec((tm, d), lambda i, c: (i, 0))
    vec = pl.BlockSpec((1, d), lambda i, c: (0, 0))
    rows_all = tm + 2 * GRID_W
    return pl.pallas_call(
        _ffn_tail_kernel,
        grid=(t // tm, nc),
        in_specs=trio(0) + trio(nc) + [pl.BlockSpec((9, fc), lambda i, c: (0, c)),
                                       pl.BlockSpec((9, fc), lambda i, c: (0, nc + c)),
                                       pl.BlockSpec((fc, d), lambda i, c: (c, 0)),
                                       tok, vec, vec],
        out_specs=tok,
        out_shape=jax.ShapeDtypeStruct((t, d), F32),
        scratch_shapes=[pltpu.VMEM((rows_all, fc), BF16), pltpu.VMEM((rows_all, fc), BF16), pltpu.VMEM((tm, d), F32)],
        compiler_params=_cparams(("parallel", "arbitrary")),
        name="ffn_tail",
    )(*([up] * 6), cw, cw, w_down, x1, gt2, g_post2)


def _pad_lora(a, axis):
    pad = [(0, 0)] * a.ndim
    pad[axis] = (0, LORA_PAD - a.shape[axis])
    return jnp.pad(a, pad)


def kernel(x, c, ctx, c_ctx, w_mod, b_mod, g_pre1, g_post1, g_pre2, g_post2, w_in, conv_rwkv, w0, w_decay_up, a0, w_iclr_up, w_gate_up, k_k, k_a, r_k, lnx_g, lnx_b, ln_v_g, ln_v_b, w_spatial, b_spatial, w_proj_a, w_proj_b, w_out, w_up, conv_ffn, w_down):
    assert x.shape[0] == 1 and w_mod.shape[0] == 1
    xt, ct = x[0], ctx[0]
    t, d = xt.shape
    tc = ct.shape[0]
    row = lambda a: a.reshape(1, -1)

    off_dec = 3 * d
    off_iclr = off_dec + 2 * DECAY_LORA
    off_gate = off_iclr + 2 * ICLR_LORA
    rwkv_cols = off_gate + GATE_LORA

    def lora_layout(a):
        parts = [_pad_lora(a[:, off_dec + k * DECAY_LORA:off_dec + (k + 1) * DECAY_LORA], 1) for k in range(2)]
        parts += [_pad_lora(a[:, off_iclr + k * ICLR_LORA:off_iclr + (k + 1) * ICLR_LORA], 1) for k in range(2)]
        parts += [a[:, off_gate:rwkv_cols], jnp.zeros((a.shape[0], LORA_COLS - 4 * LORA_PAD - GATE_LORA), a.dtype)]
        return jnp.concatenate(parts, axis=1)

    wi = w_in[0]
    w_in16 = wi.astype(BF16)
    w_lora = lora_layout(wi).astype(BF16)
    cr = conv_rwkv[0]
    cw_r, cw_k, cw_v, cw_l = cr[:, :d], cr[:, d:2 * d], cr[:, 2 * d:3 * d], lora_layout(cr)
    wdec = _pad_lora(w_decay_up[0], 1).astype(BF16)
    wiclr = _pad_lora(w_iclr_up[0], 1).astype(BF16)
    wgate = w_gate_up[0].astype(BF16)
    ones_bd = jnp.kron(jnp.eye(SCAN_G, dtype=F32), jnp.ones((HEAD, HEAD), F32)).astype(BF16)
    gch = ln_v_g.shape[1] // GMLP_GROUPS
    bsp = jnp.repeat(b_spatial[0].T, gch, axis=1)
    cw_ffn = conv_ffn[0].reshape(9, -1)

    mod = _mod_call(jnp.stack([c[0], c_ctx], axis=1), w_mod[0], row(b_mod[0]))
    sh1, sc1, gt1, sh2, sc2, gt2 = [mod[0:1, k * d:(k + 1) * d] for k in range(6)]
    csh1, csc1 = mod[1:2, 0:d], mod[1:2, d:2 * d]

    tn = 1024
    n_rkv = off_dec // tn
    n_rest = (wi.shape[1] - rwkv_cols) // tn
    n_blocks = n_rkv + n_rest + 1
    p = _inproj_call(xt, row(g_pre1[0]), sc1, sh1, w_in16, w_lora, n_rkv, rwkv_cols - off_dec, n_rkv + n_rest,
                     min(1024, t), tn)
    pc = _inproj_call(ct, row(g_pre1[0]), csc1, csh1, w_in16, w_lora, n_rkv, 0, n_rkv, min(256, tc), tn)

    targs = (cw_r, cw_k, cw_v, cw_l, w0[0], wdec, a0[0], wiclr, wgate, row(k_k[0]), row(k_a[0]), row(r_k[0]), ones_bd)
    r_c, v_c, kk_c, lw_c, kd_c, b_c, _, _ = _terms_call(pc, 6, *targs, tm=min(256, tc))
    r_x, v_x, kk_x, lw_x, kd_x, b_x, g_x, bonus_x = _terms_call(p, n_blocks - 1, *targs, tm=256)
    zero_state = jnp.zeros((d // SCAN_W, SCAN_W, SCAN_W), F32)
    ys = []
    for dirn in range(2):
        _, s_ctx = _scan_call(r_c, v_c, kk_c, lw_c, kd_c, b_c, zero_state, dirn, min(256, tc))
        y_d, _ = _scan_call(r_x, v_x, kk_x, lw_x, kd_x, b_x, s_ctx, dirn, min(512, t))
        ys.append(y_d)

    m = _merge_call(ys[0], ys[1], bonus_x, g_x, p, row(lnx_g[0]), row(lnx_b[0]), row(ln_v_g[0]), row(ln_v_b[0]),
                    w_spatial[0].astype(BF16), bsp, w_proj_a[0].astype(BF16), w_proj_b[0].astype(BF16), ones_bd,
                    tm=min(256, t))
    x1, h2 = _outproj_call(m, w_out[0], xt, gt1, row(g_post1[0]), row(g_pre2[0]), sc2, sh2,
                           tm=min(512, t))

    up = _matmul_call(h2, w_up[0], min(2048, t), 1024, BF16, "ffn_up")
    out = _ffn_tail_call(up, cw_ffn, w_down[0].astype(BF16), x1, gt2, row(g_post2[0]), tm=min(512, t), fc=512)
    return out[None]
```

```python
import functools

import jax
import jax.numpy as jnp
from jax import lax
from jax.experimental import pallas as pl
from jax.experimental.pallas import tpu as pltpu

F32 = jnp.float32
BF16 = jnp.bfloat16

RMS_EPS = 1e-6
LNX_EPS = 64e-5
LN_EPS = 1e-5
HEAD = 64
GRID_W = 64
CHUNK = 128
GMLP_GROUPS = 8
DECAY_LORA = 96
ICLR_LORA = 96
GATE_LORA = 256
LORA_PAD = 128
LORA_COLS = 1024
EXP_M05 = 0.6065306597126334

SCAN_L = 64
SCAN_G = 4
SCAN_W = SCAN_G * HEAD

VMEM_LIMIT = 56 * 1024 * 1024
LANE = 128

TILES = dict(
    mod_cols=1024,
    inproj_rows=1024, inproj_ctx_rows=256, inproj_cols=1024,
    terms_rows=256,
    scan_rows=512, scan_ctx_rows=256,
    merge_rows=256,
    outproj_rows=512,
    ffn_up_rows=2048, ffn_up_cols=1024,
    ffn_tail_rows=512, ffn_tail_cols=512,
)


def _cparams(sem):
    return pltpu.CompilerParams(dimension_semantics=sem, vmem_limit_bytes=VMEM_LIMIT)


def _dot(a, b):
    return jnp.dot(a, b, preferred_element_type=F32)


def _dot_nt(a, b):
    return lax.dot_general(a, b, (((1,), (1,)), ((), ())), preferred_element_type=F32)


def _dot_tn(a, b):
    return lax.dot_general(a, b, (((0,), (0,)), ((), ())), preferred_element_type=F32)


NEG_LOG2_E = -1.4426950408889634


def _sigmoid(x):
    return 1.0 / (1.0 + jnp.exp2(x * NEG_LOG2_E))


def _split3(x):
    h = x.astype(BF16)
    r1 = x - h.astype(F32)
    m = r1.astype(BF16)
    l = (r1 - m.astype(F32)).astype(BF16)
    return h, m, l


def _segsum(x, ones_bd):
    rows, width = x.shape
    outs = []
    for g in range(width // SCAN_W):
        xg = x[:, g * SCAN_W:(g + 1) * SCAN_W]
        h = xg.astype(BF16)
        l = (xg - h.astype(F32)).astype(BF16)
        s = _dot(jnp.concatenate([h, l], axis=0), ones_bd)
        outs.append(s[0:rows] + s[rows:2 * rows])
    return outs[0] if len(outs) == 1 else jnp.concatenate(outs, axis=1)


def _run_interleaved(gens):
    while gens:
        still = []
        for gen in gens:
            try:
                next(gen)
                still.append(gen)
            except StopIteration:
                pass
        gens = still


def _mod_kernel(cc_ref, w_ref, b_ref, o_ref):
    s = cc_ref[...]
    s = s * _sigmoid(s)
    w = w_ref[...]
    r0 = jnp.sum(s[:, 0:1] * w, axis=0, keepdims=True)
    r1 = jnp.sum(s[:, 1:2] * w, axis=0, keepdims=True)
    o_ref[...] = jnp.concatenate([r0, r1], axis=0) + b_ref[...]


def _mod_call(cc, w_mod, b_mod):
    d, n = w_mod.shape
    tn = TILES["mod_cols"]
    return pl.pallas_call(
        _mod_kernel,
        grid=(n // tn,),
        in_specs=[pl.BlockSpec((d, 2), lambda j: (0, 0)),
                  pl.BlockSpec((d, tn), lambda j: (0, j)),
                  pl.BlockSpec((1, tn), lambda j: (0, j))],
        out_specs=pl.BlockSpec((2, tn), lambda j: (0, j)),
        out_shape=jax.ShapeDtypeStruct((2, n), F32),
        compiler_params=_cparams(("parallel",)),
        name="mod",
    )(cc, w_mod, b_mod)


def _modulated_rmsnorm(x, g, sc, sh):
    ms = jnp.mean(x * x, axis=-1, keepdims=True)
    return (x * lax.rsqrt(ms + RMS_EPS) * g) * (1.0 + sc) + sh


def _inproj_kernel(x_ref, g_ref, sc_ref, sh_ref, w_ref, wl_ref, o_ref, h_ref, *, n_main):
    j = pl.program_id(1)

    @pl.when(j == 0)
    def _():
        h_ref[...] = _modulated_rmsnorm(x_ref[...], g_ref[...], sc_ref[...], sh_ref[...]).astype(BF16)

    @pl.when(j < n_main)
    def _():
        o_ref[...] = _dot(h_ref[...], w_ref[...]).astype(o_ref.dtype)

    @pl.when(j == n_main)
    def _():
        o_ref[...] = _dot(h_ref[...], wl_ref[...]).astype(o_ref.dtype)


def _inproj_call(x, g, sc, sh, w, w_lora, n_first, skip, n_main, tm, tn):
    t, d = x.shape

    assert tn % LANE == 0 and skip % LANE == 0

    def wstart(i, j):
        jj = jnp.minimum(j, n_main - 1)
        return 0, (jj * (tn // LANE) + jnp.where(jj >= n_first, skip // LANE, 0)) * LANE

    vec = pl.BlockSpec((1, d), lambda i, j: (0, 0))
    return pl.pallas_call(
        functools.partial(_inproj_kernel, n_main=n_main),
        grid=(t // tm, n_main + 1),
        in_specs=[pl.BlockSpec((tm, d), lambda i, j: (i, 0)), vec, vec, vec,
                  pl.BlockSpec((pl.Element(d), pl.Element(tn)), wstart),
                  pl.BlockSpec((d, tn), lambda i, j: (0, 0))],
        out_specs=pl.BlockSpec((tm, tn), lambda i, j: (i, j)),
        out_shape=jax.ShapeDtypeStruct((t, (n_main + 1) * tn), BF16),
        scratch_shapes=[pltpu.VMEM((tm, d), BF16)],
        compiler_params=_cparams(("parallel", "arbitrary")),
        name="inproj",
    )(x, g, sc, sh, w, w_lora)


HALO = 16


def _conv3(main, prev_blk, next_blk, w, first, last):
    main = main.astype(F32)
    tm = main.shape[0]
    rows = lax.broadcasted_iota(jnp.int32, (8, main.shape[1]), 0)
    before = jnp.where(first, 0.0, prev_blk[HALO - 1:HALO, :].astype(F32))
    after = jnp.where(last, 0.0, next_blk[0:1, :].astype(F32))
    dn = pltpu.roll(main, 1, 0)
    dn = jnp.concatenate([jnp.where(rows == 0, before, dn[0:8]), dn[8:]], axis=0)
    up = pltpu.roll(main, tm - 1, 0)
    up = jnp.concatenate([up[:tm - 8], jnp.where(rows == 7, after, up[tm - 8:])], axis=0)
    return w[0:1, :] * dn + w[1:2, :] * main + w[2:3, :] * up


def _store_groups(o_ref, x):
    for g in range(o_ref.shape[0]):
        o_ref[g] = x[:, g * SCAN_W:(g + 1) * SCAN_W].astype(o_ref.dtype)


def _terms_kernel(pr, pr_p, pr_n, pk, pk_p, pk_n, pv, pv_p, pv_n, plo, plo_p, plo_n,
                  cw_r, cw_k, cw_v, cw_l, w0, wdec, a0, wiclr, wgate, k_k, k_a, r_k, ones_bd,
                  r_o, v_o, kk_o, lw_o, kd_o, b_o, g_o, bonus_o):
    i = pl.program_id(0)
    first = i == 0
    last = i == pl.num_programs(0) - 1
    r = _conv3(pr[...], pr_p[...], pr_n[...], cw_r[...], first, last)
    k = _conv3(pk[...], pk_p[...], pk_n[...], cw_k[...], first, last)
    v = _conv3(pv[...], pv_p[...], pv_n[...], cw_v[...], first, last)
    lora = _conv3(plo[...], plo_p[...], plo_n[...], cw_l[...], first, last)
    ones = ones_bd[...]
    _store_groups(r_o, r)
    _store_groups(v_o, v)
    kk = k * k_k[...]
    kk = kk * lax.rsqrt(jnp.maximum(_segsum(kk * kk, ones), 1e-24))
    _store_groups(kk_o, kk)
    gate = _sigmoid(lora[:, 4 * LORA_PAD:4 * LORA_PAD + GATE_LORA]).astype(BF16)
    g_o[...] = _dot(gate, wgate[...]).astype(g_o.dtype)
    ksum = None
    for d in range(2):
        dec = jnp.tanh(lora[:, d * LORA_PAD:(d + 1) * LORA_PAD]).astype(BF16)
        w_pre = w0[d:d + 1, :] + _dot(dec, wdec[d])
        _store_groups(lw_o.at[d], -EXP_M05 * _sigmoid(w_pre))
        icl = lora[:, (2 + d) * LORA_PAD:(3 + d) * LORA_PAD].astype(BF16)
        a = _sigmoid(a0[d:d + 1, :] + _dot(icl, wiclr[d]))
        kd = k * (1.0 + (a - 1.0) * k_a[...])
        _store_groups(kd_o.at[d], kd)
        _store_groups(b_o.at[d], kk * a)
        ksum = kd if ksum is None else ksum + kd
    bonus_o[...] = (_segsum(r * ksum * r_k[...], ones) * v).astype(bonus_o.dtype)


def _terms_call(p, lora_block, cw_r, cw_k, cw_v, cw_l, w0, wdec, a0, wiclr, wgate, k_k, k_a, r_k, ones_bd, tm):
    t = p.shape[0]
    d = k_k.shape[1]
    nhb = t // HALO
    rh = tm // HALO

    def trio(width, cb):
        return [pl.BlockSpec((tm, width), lambda i: (i, cb)),
                pl.BlockSpec((HALO, width), lambda i: (jnp.maximum(i * rh - 1, 0), cb)),
                pl.BlockSpec((HALO, width), lambda i: (jnp.minimum((i + 1) * rh, nhb - 1), cb))]

    def full(a):
        nd = a.ndim
        return pl.BlockSpec(a.shape, lambda i: (0,) * nd)

    consts = [cw_r, cw_k, cw_v, cw_l, w0, wdec, a0, wiclr, wgate, k_k, k_a, r_k, ones_bd]
    ng = d // SCAN_W
    tok = pl.BlockSpec((tm, d), lambda i: (i, 0))
    grp = pl.BlockSpec((ng, tm, SCAN_W), lambda i: (0, i, 0))
    grp2 = pl.BlockSpec((2, ng, tm, SCAN_W), lambda i: (0, 0, i, 0))
    sd = jax.ShapeDtypeStruct((t, d), BF16)
    gsd = jax.ShapeDtypeStruct((ng, t, SCAN_W), BF16)
    gsd2 = jax.ShapeDtypeStruct((2, ng, t, SCAN_W), BF16)
    lw_sd = jax.ShapeDtypeStruct((2, ng, t, SCAN_W), F32)
    return pl.pallas_call(
        _terms_kernel,
        grid=(t // tm,),
        in_specs=trio(d, 0) + trio(d, 1) + trio(d, 2) + trio(LORA_COLS, lora_block) + [full(a) for a in consts],
        out_specs=[grp, grp, grp, grp2, grp2, grp2, tok, tok],
        out_shape=[gsd, gsd, gsd, lw_sd, gsd2, gsd2, sd, sd],
        compiler_params=_cparams(("parallel",)),
        name="terms",
    )(*([p] * 12), *consts)


def _scan_kernel(r_ref, v_ref, kk_ref, lw_ref, kd_ref, b_ref, p0_ref, y_ref, pf_ref,
                 state_ref, rm_s, yl_s, c_s, *, reverse, nck):
    L, W, G = SCAN_L, SCAN_W, SCAN_G
    i = pl.program_id(1)
    nb = pl.num_programs(1) - 1
    slot_w = i % 2
    slot_r = 1 - slot_w

    @pl.when(i == 0)
    def _():
        state_ref[...] = p0_ref[...]
        rm_s[1] = jnp.zeros(rm_s.shape[1:], rm_s.dtype)
        yl_s[1] = jnp.zeros(yl_s.shape[1:], yl_s.dtype)
        c_s[1] = jnp.zeros(c_s.shape[1:], c_s.dtype)

    row = lax.broadcasted_iota(jnp.int32, (W, W), 0)
    col = lax.broadcasted_iota(jnp.int32, (W, W), 1)
    head_shift = HEAD.bit_length() - 1
    assert 1 << head_shift == HEAD
    bd = (row >> head_shift) == (col >> head_shift)
    eye_w = row == col
    tr = lax.broadcasted_iota(jnp.int32, (L, W), 0)
    tc = lax.broadcasted_iota(jnp.int32, (L, W), 1) & (L - 1)
    t1 = lax.broadcasted_iota(jnp.int32, (L, L), 0)
    t2 = lax.broadcasted_iota(jnp.int32, (L, L), 1)
    if reverse:
        strict, incl, tinc = tc > tr, tc >= tr, t2 >= t1
    else:
        strict, incl, tinc = tc < tr, tc <= tr, t2 <= t1
    tinc = jnp.where(tinc, 1.0, 0.0).astype(BF16)
    same16 = (tr >> 4) == (tc >> 4)
    same32 = (tr >> 5) == (tc >> 5)
    eye_row = jnp.where(tr == tc, 1.0, 0.0)

    def stack(x):
        return jnp.where(bd, jnp.concatenate([x] * G, axis=0), 0.0).astype(BF16)

    def mm(a, b_stacked):
        return _dot(a.astype(BF16), b_stacked)

    def prep(c):
        sl = pl.ds(c * L, L)
        lw = lw_ref[sl, :]
        h, m, l = _split3(lw)
        cs = _dot(tinc, h) + _dot(tinc, m) + _dot(tinc, l)
        yield
        r, v, kk = r_ref[sl, :].astype(F32), v_ref[sl, :].astype(F32), kk_ref[sl, :].astype(F32)
        kd, b = kd_ref[sl, :].astype(F32), b_ref[sl, :].astype(F32)
        tot = jnp.sum(lw, axis=0, keepdims=True)
        kap = kk * jnp.exp(cs - lw)
        rh = r * jnp.exp(cs)
        ginv = jnp.exp(-cs)
        gend = jnp.exp(tot - cs)
        s_bh = stack(b * ginv)
        s_kh = stack(kd * ginv)
        kr16 = jnp.concatenate([kap, rh], axis=0).astype(BF16)
        a_b = _dot_nt(kr16, s_bh)
        a_k = _dot_nt(kr16, s_kh)
        yield
        akb = jnp.where(strict, a_b[0:L], 0.0)
        arb = jnp.where(incl, a_b[L:2 * L], 0.0)
        akk = jnp.where(strict, a_k[0:L], 0.0)
        ark = jnp.where(incl, a_k[L:2 * L], 0.0)
        dg = jnp.where(same16, akb, 0.0)
        s_v = stack(v)
        d2 = mm(dg, stack(dg))
        av = mm(jnp.concatenate([akk, ark], axis=0), s_v)
        akkv, arkv = av[0:L], av[L:2 * L]
        yield
        tm_ = eye_row - dg
        both = mm(jnp.concatenate([d2, tm_], axis=0), stack(d2))
        d4, tm_ = both[0:L], tm_ + both[L:2 * L]
        yield
        both = mm(jnp.concatenate([d4, tm_], axis=0), stack(d4))
        d8, tm_ = both[0:L], tm_ + both[L:2 * L]
        yield
        tm_ = tm_ + mm(tm_, stack(d8))
        yield
        o1 = jnp.where(same16, 0.0, jnp.where(same32, akb, 0.0))
        x1 = mm(tm_, stack(o1))
        yield
        tm_ = tm_ - mm(x1, stack(tm_))
        yield
        o2 = jnp.where(same32, 0.0, akb)
        x2 = mm(tm_, stack(o2))
        yield
        tm_ = tm_ - mm(x2, stack(tm_))
        yield
        kt = mm(tm_, stack(kap))
        u = mm(tm_, stack(akkv))
        yield
        rt = rh - mm(arb, stack(kt))
        yl = arkv - mm(arb, stack(u))
        bt16 = (b * gend).astype(BF16)
        kb16 = jnp.concatenate([(kd * gend).astype(BF16), bt16], axis=0)
        vu16 = jnp.concatenate([v, -u], axis=0).astype(BF16)
        m_raw = _dot_tn(bt16, kt.astype(BF16))
        c_raw = _dot_tn(kb16, vu16)
        yield
        m_bd = jnp.where(eye_w, jnp.broadcast_to(jnp.exp(tot), (W, W)), 0.0) - jnp.where(bd, m_raw, 0.0)
        rm_s[slot_w, c, 0:L] = rt.astype(BF16)
        rm_s[slot_w, c, L:L + W] = m_bd.astype(BF16)
        yl_s[slot_w, c] = yl
        c_s[slot_w, c] = jnp.where(bd, c_raw, 0.0)

    def state_pass():
        p = state_ref[...]
        for c in (range(nck - 1, -1, -1) if reverse else range(nck)):
            rp = _dot(rm_s[slot_r, c], p.astype(BF16))
            y_ref[pl.ds(c * L, L), :] = (rp[0:L] + yl_s[slot_r, c]).astype(y_ref.dtype)
            p = jnp.where(i > 0, rp[L:L + W] + c_s[slot_r, c], p)
            yield
        state_ref[...] = p
        pf_ref[...] = p

    @pl.when(i < nb)
    def _():
        _run_interleaved([state_pass()] + [prep(c) for c in range(nck)])

    @pl.when(i == nb)
    def _():
        _run_interleaved([state_pass()])


def _scan_call(r, v, kk, lw, kd, b, p0, d, tb):
    ng, t, _ = r.shape
    nb = t // tb
    reverse = d == 1
    nck = tb // SCAN_L

    def block(step):
        step = jnp.clip(step, 0, nb - 1)
        return nb - 1 - step if reverse else step

    shared = pl.BlockSpec((None, tb, SCAN_W), lambda g, i: (g, block(i), 0))
    per_dir = pl.BlockSpec((None, None, tb, SCAN_W), lambda g, i: (d, g, block(i), 0))
    lagged = pl.BlockSpec((None, tb, SCAN_W), lambda g, i: (g, block(i - 1), 0))
    st = pl.BlockSpec((None, SCAN_W, SCAN_W), lambda g, i: (g, 0, 0))
    return pl.pallas_call(
        functools.partial(_scan_kernel, reverse=reverse, nck=nck),
        grid=(ng, nb + 1),
        in_specs=[shared, shared, shared, per_dir, per_dir, per_dir, st],
        out_specs=[lagged, st],
        out_shape=[jax.ShapeDtypeStruct((ng, t, SCAN_W), BF16), jax.ShapeDtypeStruct((ng, SCAN_W, SCAN_W), F32)],
        scratch_shapes=[pltpu.VMEM((SCAN_W, SCAN_W), F32),
                        pltpu.VMEM((2, nck, SCAN_L + SCAN_W, SCAN_W), BF16),
                        pltpu.VMEM((2, nck, SCAN_L, SCAN_W), F32),
                        pltpu.VMEM((2, nck, SCAN_W, SCAN_W), F32)],
        compiler_params=_cparams(("parallel", "arbitrary")),
        name="scan_bwd" if reverse else "scan_fwd",
    )(r, v, kk, lw, kd, b, p0)


def _gelu(x):
    return 0.5 * x * (1.0 + lax.erf(x * 0.7071067811865476))


def _merge_kernel(y0, y1, bonus, g, uv, ga, gb, lnx_g, lnx_b, lnv_g, lnv_b, ws, bsp, wpa, wpb, ones_bd, m_o):
    ones = ones_bd[...]
    inv_n = 1.0 / HEAD
    y = jnp.concatenate([y0[g_].astype(F32) + y1[g_].astype(F32) for g_ in range(y0.shape[0])], axis=1)
    mu = _segsum(y, ones) * inv_n
    yc = y - mu
    var = _segsum(yc * yc, ones) * inv_n
    yn = yc * lax.rsqrt(var + LNX_EPS) * lnx_g[...] + lnx_b[...]
    y_rwkv = ((yn + bonus[...].astype(F32)) * g[...].astype(F32)).astype(BF16)
    branch_a = _dot(y_rwkv, wpa[...])

    z = _gelu(uv[...].astype(F32))
    half = z.shape[1] // 2
    u, vv = z[:, :half], z[:, half:]
    mu = jnp.mean(vv, axis=-1, keepdims=True)
    vc = vv - mu
    var = jnp.mean(vc * vc, axis=-1, keepdims=True)
    vn = (vc * lax.rsqrt(var + LN_EPS) * lnv_g[...] + lnv_b[...]).astype(BF16)
    gch = half // GMLP_GROUPS
    rows = []
    for ch in range(z.shape[0] // CHUNK):
        cols = [_dot(ws[gi], vn[ch * CHUNK:(ch + 1) * CHUNK, gi * gch:(gi + 1) * gch]) for gi in range(GMLP_GROUPS)]
        rows.append(jnp.concatenate(cols, axis=1) + bsp[...])
    s = rows[0] if len(rows) == 1 else jnp.concatenate(rows, axis=0)
    y_gmlp = (u * s).astype(BF16)
    branch_b = _dot(y_gmlp, wpb[...])
    m_o[...] = (_sigmoid(ga[...].astype(F32)) * branch_a + _sigmoid(gb[...].astype(F32)) * branch_b).astype(BF16)


def _merge_call(y0, y1, bonus, g, p, lnx_g, lnx_b, lnv_g, lnv_b, ws, bsp, wpa, wpb, ones_bd, tm):
    t, d = bonus.shape

    def full(a):
        nd = a.ndim
        return pl.BlockSpec(a.shape, lambda i: (0,) * nd, pipeline_mode=pl.Buffered(1))

    tok = pl.BlockSpec((tm, d), lambda i: (i, 0))
    grp = pl.BlockSpec((d // SCAN_W, tm, SCAN_W), lambda i: (0, i, 0))
    consts = [lnx_g, lnx_b, lnv_g, lnv_b, ws, bsp, wpa, wpb, ones_bd]
    return pl.pallas_call(
        _merge_kernel,
        grid=(t // tm,),
        in_specs=[grp, grp, tok, tok,
                  pl.BlockSpec((tm, d), lambda i: (i, 3)),
                  pl.BlockSpec((tm, d), lambda i: (i, 4)),
                  pl.BlockSpec((tm, d), lambda i: (i, 5)),
                  ] + [full(a) for a in consts],
        out_specs=tok,
        out_shape=jax.ShapeDtypeStruct((t, d), BF16),
        compiler_params=_cparams(("parallel",)),
        name="merge",
    )(y0, y1, bonus, g, p, p, p, *consts)


def _outproj_kernel(m, w, x, gt1, g_post1, g_pre2, sc2, sh2, x1_o, h2_o, w16):
    @pl.when(pl.program_id(0) == 0)
    def _():
        w16[...] = w[...].astype(BF16)

    mix = _dot(m[...], w16[...])
    ms = jnp.mean(mix * mix, axis=-1, keepdims=True)
    x1 = x[...] + gt1[...] * (mix * lax.rsqrt(ms + RMS_EPS) * g_post1[...])
    x1_o[...] = x1
    h2_o[...] = _modulated_rmsnorm(x1, g_pre2[...], sc2[...], sh2[...]).astype(BF16)


def _outproj_call(m, w, x, gt1, g_post1, g_pre2, sc2, sh2, tm):
    t, d = x.shape
    tok = pl.BlockSpec((tm, d), lambda i: (i, 0))
    vec = pl.BlockSpec((1, d), lambda i: (0, 0))
    return pl.pallas_call(
        _outproj_kernel,
        grid=(t // tm,),
        in_specs=[tok, pl.BlockSpec((d, d), lambda i: (0, 0), pipeline_mode=pl.Buffered(1)),
                  tok, vec, vec, vec, vec, vec],
        out_specs=[tok, tok],
        out_shape=[jax.ShapeDtypeStruct((t, d), F32), jax.ShapeDtypeStruct((t, d), BF16)],
        scratch_shapes=[pltpu.VMEM((d, d), BF16)],
        compiler_params=_cparams(("arbitrary",)),
        name="outproj",
    )(m, w, x, gt1, g_post1, g_pre2, sc2, sh2)


FFN_CONV_ROWS = 64
FFN_CONV_W = 128
FFN_DOWN_ROWS = 512


def _matmul_kernel(x_ref, w_ref, o_ref, w16):
    @pl.when(pl.program_id(1) == 0)
    def _():
        w16[...] = w_ref[...].astype(BF16)

    o_ref[...] = _dot(x_ref[...], w16[...]).astype(o_ref.dtype)


def _matmul_call(x, w, tm, tn, out_dtype, name):
    m, k = x.shape
    n = w.shape[1]
    return pl.pallas_call(
        _matmul_kernel,
        grid=(n // tn, m // tm),
        in_specs=[pl.BlockSpec((tm, k), lambda j, i: (i, 0)), pl.BlockSpec((k, tn), lambda j, i: (0, j))],
        out_specs=pl.BlockSpec((tm, tn), lambda j, i: (i, j)),
        out_shape=jax.ShapeDtypeStruct((m, n), out_dtype),
        scratch_shapes=[pltpu.VMEM((k, tn), BF16)],
        compiler_params=_cparams(("parallel", "arbitrary")),
        name=name,
    )(x, w)


def _ffn_tail_kernel(ua_m, ua_p, ua_n, ub_m, ub_p, ub_n, cwa, cwb, wd, x1, gt2, g_post2, o_ref, abuf, bbuf, acc):
    i = pl.program_id(0)
    c = pl.program_id(1)
    tm = ua_m.shape[0]
    fc = ua_m.shape[1]
    first = i == 0
    last = i == pl.num_programs(0) - 1

    @pl.when(c == 0)
    def _():
        acc[...] = jnp.zeros_like(acc)

    def fill(buf, main, prev, nxt):
        zero = jnp.zeros(prev.shape, prev.dtype)
        buf[0:GRID_W, :] = jnp.where(first, zero, prev[...])
        buf[GRID_W:GRID_W + tm, :] = main[...]
        buf[GRID_W + tm:GRID_W + tm + GRID_W, :] = jnp.where(last, zero, nxt[...])

    fill(abuf, ua_m, ua_p, ua_n)
    fill(bbuf, ub_m, ub_p, ub_n)

    nr = FFN_CONV_ROWS
    rt_ = lax.broadcasted_iota(jnp.int32, (nr, 3 * nr), 0)
    ct_ = lax.broadcasted_iota(jnp.int32, (nr, 3 * nr), 1)
    gcol = rt_ & (GRID_W - 1)
    hit = ((ct_ == rt_ - 1) & (gcol != 0)) | (ct_ == rt_ + nr) | ((ct_ == rt_ + 2 * nr + 1) & (gcol != GRID_W - 1))
    shift_add = jnp.where(hit, 1.0, 0.0).astype(BF16)

    def conv_rows(u_ref, cw_ref, r0):
        cw = cw_ref[...].astype(BF16)
        taps = [u_ref[r0 + dr * GRID_W:r0 + dr * GRID_W + nr, :] for dr in range(3)]
        col_sums = [cw[dc:dc + 1, :] * taps[0] + cw[3 + dc:4 + dc, :] * taps[1] + cw[6 + dc:7 + dc, :] * taps[2]
                    for dc in range(3)]
        return _dot(shift_add, jnp.concatenate(col_sums, axis=0))

    for m0 in range(0, tm, FFN_DOWN_ROWS):
        row_blocks = []
        for r0 in range(m0, m0 + FFN_DOWN_ROWS, nr):
            a = conv_rows(abuf, cwa, r0)
            b = conv_rows(bbuf, cwb, r0)
            row_blocks.append((a * _sigmoid(a) * b).astype(BF16))
        acc[m0:m0 + FFN_DOWN_ROWS, :] += _dot(jnp.concatenate(row_blocks, axis=0), wd[...])

    @pl.when(c == pl.num_programs(1) - 1)
    def _():
        f = acc[...]
        ms = jnp.mean(f * f, axis=-1, keepdims=True)
        o_ref[...] = x1[...] + gt2[...] * (f * lax.rsqrt(ms + RMS_EPS) * g_post2[...])


def _ffn_tail_call(up, cw, w_down, x1, gt2, g_post2, tm, fc):
    t, d = x1.shape
    dff = w_down.shape[0]
    nc = dff // fc
    hb = tm // GRID_W
    nhb = t // GRID_W

    def trio(off):
        return [pl.BlockSpec((tm, fc), lambda i, c: (i, off + c)),
                pl.BlockSpec((GRID_W, fc), lambda i, c: (jnp.maximum(i * hb - 1, 0), off + c)),
                pl.BlockSpec((GRID_W, fc), lambda i, c: (jnp.minimum((i + 1) * hb, nhb - 1), off + c))]

    tok = pl.BlockSpec((tm, d), lambda i, c: (i, 0))
    vec = pl.BlockSpec((1, d), lambda i, c: (0, 0))
    rows_all = tm + 2 * GRID_W
    return pl.pallas_call(
        _ffn_tail_kernel,
        grid=(t // tm, nc),
        in_specs=trio(0) + trio(nc) + [pl.BlockSpec((9, fc), lambda i, c: (0, c)),
                                       pl.BlockSpec((9, fc), lambda i, c: (0, nc + c)),
                                       pl.BlockSpec((fc, d), lambda i, c: (c, 0)),
                                       tok, vec, vec],
        out_specs=tok,
        out_shape=jax.ShapeDtypeStruct((t, d), F32),
        scratch_shapes=[pltpu.VMEM((rows_all, fc), BF16), pltpu.VMEM((rows_all, fc), BF16), pltpu.VMEM((tm, d), F32)],
        compiler_params=_cparams(("parallel", "arbitrary")),
        name="ffn_tail",
    )(*([up] * 6), cw, cw, w_down, x1, gt2, g_post2)


def _pad_lora(a, axis):
    pad = [(0, 0)] * a.ndim
    pad[axis] = (0, LORA_PAD - a.shape[axis])
    return jnp.pad(a, pad)


def kernel(x, c, ctx, c_ctx, w_mod, b_mod, g_pre1, g_post1, g_pre2, g_post2, w_in, conv_rwkv, w0, w_decay_up, a0, w_iclr_up, w_gate_up, k_k, k_a, r_k, lnx_g, lnx_b, ln_v_g, ln_v_b, w_spatial, b_spatial, w_proj_a, w_proj_b, w_out, w_up, conv_ffn, w_down):
    assert x.shape[0] == 1 and w_mod.shape[0] == 1
    xt, ct = x[0], ctx[0]
    t, d = xt.shape
    tc = ct.shape[0]
    row = lambda a: a.reshape(1, -1)

    off_dec = 3 * d
    off_iclr = off_dec + 2 * DECAY_LORA
    off_gate = off_iclr + 2 * ICLR_LORA
    rwkv_cols = off_gate + GATE_LORA

    def lora_layout(a):
        parts = [_pad_lora(a[:, off_dec + k * DECAY_LORA:off_dec + (k + 1) * DECAY_LORA], 1) for k in range(2)]
        parts += [_pad_lora(a[:, off_iclr + k * ICLR_LORA:off_iclr + (k + 1) * ICLR_LORA], 1) for k in range(2)]
        parts += [a[:, off_gate:rwkv_cols], jnp.zeros((a.shape[0], LORA_COLS - 4 * LORA_PAD - GATE_LORA), a.dtype)]
        return jnp.concatenate(parts, axis=1)

    wi = w_in[0]
    w_in16 = wi.astype(BF16)
    w_lora = lora_layout(wi).astype(BF16)
    cr = conv_rwkv[0]
    cw_r, cw_k, cw_v, cw_l = cr[:, :d], cr[:, d:2 * d], cr[:, 2 * d:3 * d], lora_layout(cr)
    wdec = _pad_lora(w_decay_up[0], 1).astype(BF16)
    wiclr = _pad_lora(w_iclr_up[0], 1).astype(BF16)
    wgate = w_gate_up[0].astype(BF16)
    ones_bd = jnp.kron(jnp.eye(SCAN_G, dtype=F32), jnp.ones((HEAD, HEAD), F32)).astype(BF16)
    gch = ln_v_g.shape[1] // GMLP_GROUPS
    bsp = jnp.repeat(b_spatial[0].T, gch, axis=1)
    cw_ffn = conv_ffn[0].reshape(9, -1)

    mod = _mod_call(jnp.stack([c[0], c_ctx], axis=1), w_mod[0], row(b_mod[0]))
    sh1, sc1, gt1, sh2, sc2, gt2 = [mod[0:1, k * d:(k + 1) * d] for k in range(6)]
    csh1, csc1 = mod[1:2, 0:d], mod[1:2, d:2 * d]

    tile = TILES
    tn = tile["inproj_cols"]
    n_rkv = off_dec // tn
    n_rest = (wi.shape[1] - rwkv_cols) // tn
    n_blocks = n_rkv + n_rest + 1
    p = _inproj_call(xt, row(g_pre1[0]), sc1, sh1, w_in16, w_lora, n_rkv, rwkv_cols - off_dec, n_rkv + n_rest,
                     min(tile["inproj_rows"], t), tn)
    pc = _inproj_call(ct, row(g_pre1[0]), csc1, csh1, w_in16, w_lora, n_rkv, 0, n_rkv,
                      min(tile["inproj_ctx_rows"], tc), tn)

    targs = (cw_r, cw_k, cw_v, cw_l, w0[0], wdec, a0[0], wiclr, wgate, row(k_k[0]), row(k_a[0]), row(r_k[0]), ones_bd)
    r_c, v_c, kk_c, lw_c, kd_c, b_c, _, _ = _terms_call(pc, n_rkv, *targs, tm=min(tile["terms_rows"], tc))
    r_x, v_x, kk_x, lw_x, kd_x, b_x, g_x, bonus_x = _terms_call(p, n_blocks - 1, *targs,
                                                                tm=min(tile["terms_rows"], t))
    zero_state = jnp.zeros((d // SCAN_W, SCAN_W, SCAN_W), F32)
    ys = []
    for dirn in range(2):
        _, s_ctx = _scan_call(r_c, v_c, kk_c, lw_c, kd_c, b_c, zero_state, dirn, min(tile["scan_ctx_rows"], tc))
        y_d, _ = _scan_call(r_x, v_x, kk_x, lw_x, kd_x, b_x, s_ctx, dirn, min(tile["scan_rows"], t))
        ys.append(y_d)

    m = _merge_call(ys[0], ys[1], bonus_x, g_x, p, row(lnx_g[0]), row(lnx_b[0]), row(ln_v_g[0]), row(ln_v_b[0]),
                    w_spatial[0].astype(BF16), bsp, w_proj_a[0].astype(BF16), w_proj_b[0].astype(BF16), ones_bd,
                    tm=min(tile["merge_rows"], t))
    x1, h2 = _outproj_call(m, w_out[0], xt, gt1, row(g_post1[0]), row(g_pre2[0]), sc2, sh2,
                           tm=min(tile["outproj_rows"], t))

    up = _matmul_call(h2, w_up[0], min(tile["ffn_up_rows"], t), tile["ffn_up_cols"], BF16, "ffn_up")
    out = _ffn_tail_call(up, cw_ffn, w_down[0].astype(BF16), x1, gt2, row(g_post2[0]),
                         tm=min(tile["ffn_tail_rows"], t), fc=tile["ffn_tail_cols"])
    return out[None]
```

```python
import functools

import jax
import jax.numpy as jnp
from jax import lax
from jax.experimental import pallas as pl
from jax.experimental.pallas import tpu as pltpu

F32 = jnp.float32
BF16 = jnp.bfloat16

RMS_EPS = 1e-6
LNX_EPS = 64e-5
LN_EPS = 1e-5
HEAD = 64
GRID_W = 64
CHUNK = 128
GMLP_GROUPS = 8
DECAY_LORA = 96
ICLR_LORA = 96
GATE_LORA = 256
LORA_PAD = 128
LORA_COLS = 1024
EXP_M05 = 0.6065306597126334

SCAN_L = 64
SCAN_G = 4
SCAN_W = SCAN_G * HEAD

VMEM_LIMIT = 56 * 1024 * 1024
LANE = 128

TILES = dict(
    mod_cols=1024,
    prenorm_rows=512,
    inproj_rows=1024, inproj_ctx_rows=256, inproj_cols=1024,
    terms_rows=256,
    scan_rows=512, scan_ctx_rows=256,
    merge_rows=256,
    outproj_rows=512,
    ffn_up_rows=2048, ffn_up_cols=1024,
    ffn_tail_rows=512, ffn_tail_cols=512,
)


def _cparams(sem):
    return pltpu.CompilerParams(dimension_semantics=sem, vmem_limit_bytes=VMEM_LIMIT)


def _dot(a, b):
    return jnp.dot(a, b, preferred_element_type=F32)


def _dot_nt(a, b):
    return lax.dot_general(a, b, (((1,), (1,)), ((), ())), preferred_element_type=F32)


def _dot_tn(a, b):
    return lax.dot_general(a, b, (((0,), (0,)), ((), ())), preferred_element_type=F32)


NEG_LOG2_E = -1.4426950408889634


def _sigmoid(x):
    return 1.0 / (1.0 + jnp.exp2(x * NEG_LOG2_E))


def _split3(x):
    h = x.astype(BF16)
    r1 = x - h.astype(F32)
    m = r1.astype(BF16)
    l = (r1 - m.astype(F32)).astype(BF16)
    return h, m, l


def _segsum(x, ones_bd):
    rows, width = x.shape
    outs = []
    for g in range(width // SCAN_W):
        xg = x[:, g * SCAN_W:(g + 1) * SCAN_W]
        h = xg.astype(BF16)
        l = (xg - h.astype(F32)).astype(BF16)
        s = _dot(jnp.concatenate([h, l], axis=0), ones_bd)
        outs.append(s[0:rows] + s[rows:2 * rows])
    return outs[0] if len(outs) == 1 else jnp.concatenate(outs, axis=1)


def _run_interleaved(gens):
    while gens:
        still = []
        for gen in gens:
            try:
                next(gen)
                still.append(gen)
            except StopIteration:
                pass
        gens = still


def _mod_kernel(cc_ref, w_ref, b_ref, o_ref):
    s = cc_ref[...]
    s = s * _sigmoid(s)
    w = w_ref[...]
    r0 = jnp.sum(s[:, 0:1] * w, axis=0, keepdims=True)
    r1 = jnp.sum(s[:, 1:2] * w, axis=0, keepdims=True)
    o_ref[...] = jnp.concatenate([r0, r1], axis=0) + b_ref[...]


def _mod_call(cc, w_mod, b_mod):
    d, n = w_mod.shape
    tn = TILES["mod_cols"]
    return pl.pallas_call(
        _mod_kernel,
        grid=(n // tn,),
        in_specs=[pl.BlockSpec((d, 2), lambda j: (0, 0)),
                  pl.BlockSpec((d, tn), lambda j: (0, j)),
                  pl.BlockSpec((1, tn), lambda j: (0, j))],
        out_specs=pl.BlockSpec((2, tn), lambda j: (0, j)),
        out_shape=jax.ShapeDtypeStruct((2, n), F32),
        compiler_params=_cparams(("parallel",)),
        name="mod",
    )(cc, w_mod, b_mod)


def _modulated_rmsnorm(x, g, sc, sh):
    ms = jnp.mean(x * x, axis=-1, keepdims=True)
    return (x * lax.rsqrt(ms + RMS_EPS) * g) * (1.0 + sc) + sh


def _prenorm_kernel(x_ref, g_ref, sc_ref, sh_ref, o_ref):
    o_ref[...] = _modulated_rmsnorm(x_ref[...], g_ref[...], sc_ref[...], sh_ref[...]).astype(o_ref.dtype)


def _prenorm_call(x, g, sc, sh, tm):
    t, d = x.shape
    tok = pl.BlockSpec((tm, d), lambda i: (i, 0))
    vec = pl.BlockSpec((1, d), lambda i: (0, 0))
    return pl.pallas_call(
        _prenorm_kernel,
        grid=(t // tm,),
        in_specs=[tok, vec, vec, vec],
        out_specs=tok,
        out_shape=jax.ShapeDtypeStruct((t, d), BF16),
        compiler_params=_cparams(("parallel",)),
        name="prenorm",
    )(x, g, sc, sh)


def _inproj_kernel(h_ref, w_ref, wl_ref, o_ref, w16, *, n_main):
    j = pl.program_id(0)
    first_row_tile = pl.program_id(1) == 0

    @pl.when(jnp.logical_and(first_row_tile, j < n_main))
    def _():
        w16[...] = w_ref[...].astype(BF16)

    @pl.when(jnp.logical_and(first_row_tile, j == n_main))
    def _():
        w16[...] = wl_ref[...]

    o_ref[...] = _dot(h_ref[...], w16[...]).astype(o_ref.dtype)


def _inproj_call(h, w, w_lora, n_first, skip, n_main, tm, tn):
    t, d = h.shape
    assert tn % LANE == 0 and skip % LANE == 0

    def wstart(j, i):
        jj = jnp.minimum(j, n_main - 1)
        return 0, (jj * (tn // LANE) + jnp.where(jj >= n_first, skip // LANE, 0)) * LANE

    return pl.pallas_call(
        functools.partial(_inproj_kernel, n_main=n_main),
        grid=(n_main + 1, t // tm),
        in_specs=[pl.BlockSpec((tm, d), lambda j, i: (i, 0)),
                  pl.BlockSpec((pl.Element(d), pl.Element(tn)), wstart),
                  pl.BlockSpec((d, tn), lambda j, i: (0, 0), pipeline_mode=pl.Buffered(1))],
        out_specs=pl.BlockSpec((tm, tn), lambda j, i: (i, j)),
        out_shape=jax.ShapeDtypeStruct((t, (n_main + 1) * tn), BF16),
        scratch_shapes=[pltpu.VMEM((d, tn), BF16)],
        compiler_params=_cparams(("parallel", "arbitrary")),
        name="inproj",
    )(h, w, w_lora)


HALO = 16


def _conv3(main, prev_blk, next_blk, w, first, last):
    main = main.astype(F32)
    tm = main.shape[0]
    rows = lax.broadcasted_iota(jnp.int32, (8, main.shape[1]), 0)
    before = jnp.where(first, 0.0, prev_blk[HALO - 1:HALO, :].astype(F32))
    after = jnp.where(last, 0.0, next_blk[0:1, :].astype(F32))
    dn = pltpu.roll(main, 1, 0)
    dn = jnp.concatenate([jnp.where(rows == 0, before, dn[0:8]), dn[8:]], axis=0)
    up = pltpu.roll(main, tm - 1, 0)
    up = jnp.concatenate([up[:tm - 8], jnp.where(rows == 7, after, up[tm - 8:])], axis=0)
    return w[0:1, :] * dn + w[1:2, :] * main + w[2:3, :] * up


def _store_groups(o_ref, x):
    for g in range(o_ref.shape[0]):
        o_ref[g] = x[:, g * SCAN_W:(g + 1) * SCAN_W].astype(o_ref.dtype)


def _terms_kernel(pr, pr_p, pr_n, pk, pk_p, pk_n, pv, pv_p, pv_n, plo, plo_p, plo_n,
                  cw_r, cw_k, cw_v, cw_l, w0, wdec, a0, wiclr, wgate, k_k, k_a, r_k, ones_bd,
                  r_o, v_o, kk_o, lw_o, kd_o, b_o, g_o, bonus_o):
    i = pl.program_id(0)
    first = i == 0
    last = i == pl.num_programs(0) - 1
    r = _conv3(pr[...], pr_p[...], pr_n[...], cw_r[...], first, last)
    k = _conv3(pk[...], pk_p[...], pk_n[...], cw_k[...], first, last)
    v = _conv3(pv[...], pv_p[...], pv_n[...], cw_v[...], first, last)
    lora = _conv3(plo[...], plo_p[...], plo_n[...], cw_l[...], first, last)
    ones = ones_bd[...]
    _store_groups(r_o, r)
    _store_groups(v_o, v)
    kk = k * k_k[...]
    kk = kk * lax.rsqrt(jnp.maximum(_segsum(kk * kk, ones), 1e-24))
    _store_groups(kk_o, kk)
    gate = _sigmoid(lora[:, 4 * LORA_PAD:4 * LORA_PAD + GATE_LORA]).astype(BF16)
    g_o[...] = _dot(gate, wgate[...]).astype(g_o.dtype)
    ksum = None
    for d in range(2):
        dec = jnp.tanh(lora[:, d * LORA_PAD:(d + 1) * LORA_PAD]).astype(BF16)
        w_pre = w0[d:d + 1, :] + _dot(dec, wdec[d])
        _store_groups(lw_o.at[d], -EXP_M05 * _sigmoid(w_pre))
        icl = lora[:, (2 + d) * LORA_PAD:(3 + d) * LORA_PAD].astype(BF16)
        a = _sigmoid(a0[d:d + 1, :] + _dot(icl, wiclr[d]))
        kd = k * (1.0 + (a - 1.0) * k_a[...])
        _store_groups(kd_o.at[d], kd)
        _store_groups(b_o.at[d], kk * a)
        ksum = kd if ksum is None else ksum + kd
    bonus_o[...] = (_segsum(r * ksum * r_k[...], ones) * v).astype(bonus_o.dtype)


def _terms_call(p, lora_block, cw_r, cw_k, cw_v, cw_l, w0, wdec, a0, wiclr, wgate, k_k, k_a, r_k, ones_bd, tm):
    t = p.shape[0]
    d = k_k.shape[1]
    nhb = t // HALO
    rh = tm // HALO

    def trio(width, cb):
        return [pl.BlockSpec((tm, width), lambda i: (i, cb)),
                pl.BlockSpec((HALO, width), lambda i: (jnp.maximum(i * rh - 1, 0), cb)),
                pl.BlockSpec((HALO, width), lambda i: (jnp.minimum((i + 1) * rh, nhb - 1), cb))]

    def full(a):
        nd = a.ndim
        return pl.BlockSpec(a.shape, lambda i: (0,) * nd)

    consts = [cw_r, cw_k, cw_v, cw_l, w0, wdec, a0, wiclr, wgate, k_k, k_a, r_k, ones_bd]
    ng = d // SCAN_W
    tok = pl.BlockSpec((tm, d), lambda i: (i, 0))
    grp = pl.BlockSpec((ng, tm, SCAN_W), lambda i: (0, i, 0))
    grp2 = pl.BlockSpec((2, ng, tm, SCAN_W), lambda i: (0, 0, i, 0))
    sd = jax.ShapeDtypeStruct((t, d), BF16)
    gsd = jax.ShapeDtypeStruct((ng, t, SCAN_W), BF16)
    gsd2 = jax.ShapeDtypeStruct((2, ng, t, SCAN_W), BF16)
    lw_sd = jax.ShapeDtypeStruct((2, ng, t, SCAN_W), F32)
    return pl.pallas_call(
        _terms_kernel,
        grid=(t // tm,),
        in_specs=trio(d, 0) + trio(d, 1) + trio(d, 2) + trio(LORA_COLS, lora_block) + [full(a) for a in consts],
        out_specs=[grp, grp, grp, grp2, grp2, grp2, tok, tok],
        out_shape=[gsd, gsd, gsd, lw_sd, gsd2, gsd2, sd, sd],
        compiler_params=_cparams(("parallel",)),
        name="terms",
    )(*([p] * 12), *consts)


def _scan_kernel(r_ref, v_ref, kk_ref, lw_ref, kd_ref, b_ref, p0_ref, y_ref, pf_ref,
                 state_ref, rm_s, yl_s, c_s, *, reverse, nck):
    L, W, G = SCAN_L, SCAN_W, SCAN_G
    i = pl.program_id(1)
    nb = pl.num_programs(1) - 1
    slot_w = i % 2
    slot_r = 1 - slot_w

    @pl.when(i == 0)
    def _():
        state_ref[...] = p0_ref[...]
        rm_s[1] = jnp.zeros(rm_s.shape[1:], rm_s.dtype)
        yl_s[1] = jnp.zeros(yl_s.shape[1:], yl_s.dtype)
        c_s[1] = jnp.zeros(c_s.shape[1:], c_s.dtype)

    row = lax.broadcasted_iota(jnp.int32, (W, W), 0)
    col = lax.broadcasted_iota(jnp.int32, (W, W), 1)
    head_shift = HEAD.bit_length() - 1
    assert 1 << head_shift == HEAD
    bd = (row >> head_shift) == (col >> head_shift)
    eye_w = row == col
    tr = lax.broadcasted_iota(jnp.int32, (L, W), 0)
    tc = lax.broadcasted_iota(jnp.int32, (L, W), 1) & (L - 1)
    t1 = lax.broadcasted_iota(jnp.int32, (L, L), 0)
    t2 = lax.broadcasted_iota(jnp.int32, (L, L), 1)
    if reverse:
        strict, incl, tinc = tc > tr, tc >= tr, t2 >= t1
    else:
        strict, incl, tinc = tc < tr, tc <= tr, t2 <= t1
    tinc = jnp.where(tinc, 1.0, 0.0).astype(BF16)
    same16 = (tr >> 4) == (tc >> 4)
    same32 = (tr >> 5) == (tc >> 5)
    eye_row = jnp.where(tr == tc, 1.0, 0.0)

    def stack(x):
        return jnp.where(bd, jnp.concatenate([x] * G, axis=0), 0.0).astype(BF16)

    def mm(a, b_stacked):
        return _dot(a.astype(BF16), b_stacked)

    def prep(c):
        sl = pl.ds(c * L, L)
        lw = lw_ref[sl, :]
        h, m, l = _split3(lw)
        cs = _dot(tinc, h) + _dot(tinc, m) + _dot(tinc, l)
        yield
        r, v, kk = r_ref[sl, :].astype(F32), v_ref[sl, :].astype(F32), kk_ref[sl, :].astype(F32)
        kd, b = kd_ref[sl, :].astype(F32), b_ref[sl, :].astype(F32)
        tot = jnp.sum(lw, axis=0, keepdims=True)
        kap = kk * jnp.exp(cs - lw)
        rh = r * jnp.exp(cs)
        ginv = jnp.exp(-cs)
        gend = jnp.exp(tot - cs)
        s_bh = stack(b * ginv)
        s_kh = stack(kd * ginv)
        kr16 = jnp.concatenate([kap, rh], axis=0).astype(BF16)
        a_b = _dot_nt(kr16, s_bh)
        a_k = _dot_nt(kr16, s_kh)
        yield
        akb = jnp.where(strict, a_b[0:L], 0.0)
        arb = jnp.where(incl, a_b[L:2 * L], 0.0)
        akk = jnp.where(strict, a_k[0:L], 0.0)
        ark = jnp.where(incl, a_k[L:2 * L], 0.0)
        dg = jnp.where(same16, akb, 0.0)
        s_v = stack(v)
        d2 = mm(dg, stack(dg))
        av = mm(jnp.concatenate([akk, ark], axis=0), s_v)
        akkv, arkv = av[0:L], av[L:2 * L]
        yield
        tm_ = eye_row - dg
        both = mm(jnp.concatenate([d2, tm_], axis=0), stack(d2))
        d4, tm_ = both[0:L], tm_ + both[L:2 * L]
        yield
        both = mm(jnp.concatenate([d4, tm_], axis=0), stack(d4))
        d8, tm_ = both[0:L], tm_ + both[L:2 * L]
        yield
        tm_ = tm_ + mm(tm_, stack(d8))
        yield
        o1 = jnp.where(same16, 0.0, jnp.where(same32, akb, 0.0))
        x1 = mm(tm_, stack(o1))
        yield
        tm_ = tm_ - mm(x1, stack(tm_))
        yield
        o2 = jnp.where(same32, 0.0, akb)
        x2 = mm(tm_, stack(o2))
        yield
        tm_ = tm_ - mm(x2, stack(tm_))
        yield
        kt = mm(tm_, stack(kap))
        u = mm(tm_, stack(akkv))
        yield
        rt = rh - mm(arb, stack(kt))
        yl = arkv - mm(arb, stack(u))
        bt16 = (b * gend).astype(BF16)
        kb16 = jnp.concatenate([(kd * gend).astype(BF16), bt16], axis=0)
        vu16 = jnp.concatenate([v, -u], axis=0).astype(BF16)
        m_raw = _dot_tn(bt16, kt.astype(BF16))
        c_raw = _dot_tn(kb16, vu16)
        yield
        m_bd = jnp.where(eye_w, jnp.broadcast_to(jnp.exp(tot), (W, W)), 0.0) - jnp.where(bd, m_raw, 0.0)
        rm_s[slot_w, c, 0:L] = rt.astype(BF16)
        rm_s[slot_w, c, L:L + W] = m_bd.astype(BF16)
        yl_s[slot_w, c] = yl
        c_s[slot_w, c] = jnp.where(bd, c_raw, 0.0)

    def state_pass():
        p = state_ref[...]
        for c in (range(nck - 1, -1, -1) if reverse else range(nck)):
            rp = _dot(rm_s[slot_r, c], p.astype(BF16))
            y_ref[pl.ds(c * L, L), :] = (rp[0:L] + yl_s[slot_r, c]).astype(y_ref.dtype)
            p = jnp.where(i > 0, rp[L:L + W] + c_s[slot_r, c], p)
            yield
        state_ref[...] = p
        pf_ref[...] = p

    @pl.when(i < nb)
    def _():
        _run_interleaved([state_pass()] + [prep(c) for c in range(nck)])

    @pl.when(i == nb)
    def _():
        _run_interleaved([state_pass()])


def _scan_call(r, v, kk, lw, kd, b, p0, d, tb):
    ng, t, _ = r.shape
    nb = t // tb
    reverse = d == 1
    nck = tb // SCAN_L

    def block(step):
        step = jnp.clip(step, 0, nb - 1)
        return nb - 1 - step if reverse else step

    shared = pl.BlockSpec((None, tb, SCAN_W), lambda g, i: (g, block(i), 0))
    per_dir = pl.BlockSpec((None, None, tb, SCAN_W), lambda g, i: (d, g, block(i), 0))
    lagged = pl.BlockSpec((None, tb, SCAN_W), lambda g, i: (g, block(i - 1), 0))
    st = pl.BlockSpec((None, SCAN_W, SCAN_W), lambda g, i: (g, 0, 0))
    return pl.pallas_call(
        functools.partial(_scan_kernel, reverse=reverse, nck=nck),
        grid=(ng, nb + 1),
        in_specs=[shared, shared, shared, per_dir, per_dir, per_dir, st],
        out_specs=[lagged, st],
        out_shape=[jax.ShapeDtypeStruct((ng, t, SCAN_W), BF16), jax.ShapeDtypeStruct((ng, SCAN_W, SCAN_W), F32)],
        scratch_shapes=[pltpu.VMEM((SCAN_W, SCAN_W), F32),
                        pltpu.VMEM((2, nck, SCAN_L + SCAN_W, SCAN_W), BF16),
                        pltpu.VMEM((2, nck, SCAN_L, SCAN_W), F32),
                        pltpu.VMEM((2, nck, SCAN_W, SCAN_W), F32)],
        compiler_params=_cparams(("parallel", "arbitrary")),
        name="scan_bwd" if reverse else "scan_fwd",
    )(r, v, kk, lw, kd, b, p0)


def _gelu(x):
    return 0.5 * x * (1.0 + lax.erf(x * 0.7071067811865476))


def _merge_kernel(y0, y1, bonus, g, uv, ga, gb, lnx_g, lnx_b, lnv_g, lnv_b, ws, bsp, wpa, wpb, ones_bd, m_o):
    ones = ones_bd[...]
    inv_n = 1.0 / HEAD
    y = jnp.concatenate([y0[g_].astype(F32) + y1[g_].astype(F32) for g_ in range(y0.shape[0])], axis=1)
    mu = _segsum(y, ones) * inv_n
    yc = y - mu
    var = _segsum(yc * yc, ones) * inv_n
    yn = yc * lax.rsqrt(var + LNX_EPS) * lnx_g[...] + lnx_b[...]
    y_rwkv = ((yn + bonus[...].astype(F32)) * g[...].astype(F32)).astype(BF16)
    branch_a = _dot(y_rwkv, wpa[...])

    z = _gelu(uv[...].astype(F32))
    half = z.shape[1] // 2
    u, vv = z[:, :half], z[:, half:]
    mu = jnp.mean(vv, axis=-1, keepdims=True)
    vc = vv - mu
    var = jnp.mean(vc * vc, axis=-1, keepdims=True)
    vn = (vc * lax.rsqrt(var + LN_EPS) * lnv_g[...] + lnv_b[...]).astype(BF16)
    gch = half // GMLP_GROUPS
    rows = []
    for ch in range(z.shape[0] // CHUNK):
        cols = [_dot(ws[gi], vn[ch * CHUNK:(ch + 1) * CHUNK, gi * gch:(gi + 1) * gch]) for gi in range(GMLP_GROUPS)]
        rows.append(jnp.concatenate(cols, axis=1) + bsp[...])
    s = rows[0] if len(rows) == 1 else jnp.concatenate(rows, axis=0)
    y_gmlp = (u * s).astype(BF16)
    branch_b = _dot(y_gmlp, wpb[...])
    m_o[...] = (_sigmoid(ga[...].astype(F32)) * branch_a + _sigmoid(gb[...].astype(F32)) * branch_b).astype(BF16)


def _merge_call(y0, y1, bonus, g, p, lnx_g, lnx_b, lnv_g, lnv_b, ws, bsp, wpa, wpb, ones_bd, tm):
    t, d = bonus.shape

    def full(a):
        nd = a.ndim
        return pl.BlockSpec(a.shape, lambda i: (0,) * nd, pipeline_mode=pl.Buffered(1))

    tok = pl.BlockSpec((tm, d), lambda i: (i, 0))
    grp = pl.BlockSpec((d // SCAN_W, tm, SCAN_W), lambda i: (0, i, 0))
    consts = [lnx_g, lnx_b, lnv_g, lnv_b, ws, bsp, wpa, wpb, ones_bd]
    return pl.pallas_call(
        _merge_kernel,
        grid=(t // tm,),
        in_specs=[grp, grp, tok, tok,
                  pl.BlockSpec((tm, d), lambda i: (i, 3)),
                  pl.BlockSpec((tm, d), lambda i: (i, 4)),
                  pl.BlockSpec((tm, d), lambda i: (i, 5)),
                  ] + [full(a) for a in consts],
        out_specs=tok,
        out_shape=jax.ShapeDtypeStruct((t, d), BF16),
        compiler_params=_cparams(("parallel",)),
        name="merge",
    )(y0, y1, bonus, g, p, p, p, *consts)


def _outproj_kernel(m, w, x, gt1, g_post1, g_pre2, sc2, sh2, x1_o, h2_o, w16):
    @pl.when(pl.program_id(0) == 0)
    def _():
        w16[...] = w[...].astype(BF16)

    mix = _dot(m[...], w16[...])
    ms = jnp.mean(mix * mix, axis=-1, keepdims=True)
    x1 = x[...] + gt1[...] * (mix * lax.rsqrt(ms + RMS_EPS) * g_post1[...])
    x1_o[...] = x1
    h2_o[...] = _modulated_rmsnorm(x1, g_pre2[...], sc2[...], sh2[...]).astype(BF16)


def _outproj_call(m, w, x, gt1, g_post1, g_pre2, sc2, sh2, tm):
    t, d = x.shape
    tok = pl.BlockSpec((tm, d), lambda i: (i, 0))
    vec = pl.BlockSpec((1, d), lambda i: (0, 0))
    return pl.pallas_call(
        _outproj_kernel,
        grid=(t // tm,),
        in_specs=[tok, pl.BlockSpec((d, d), lambda i: (0, 0), pipeline_mode=pl.Buffered(1)),
                  tok, vec, vec, vec, vec, vec],
        out_specs=[tok, tok],
        out_shape=[jax.ShapeDtypeStruct((t, d), F32), jax.ShapeDtypeStruct((t, d), BF16)],
        scratch_shapes=[pltpu.VMEM((d, d), BF16)],
        compiler_params=_cparams(("arbitrary",)),
        name="outproj",
    )(m, w, x, gt1, g_post1, g_pre2, sc2, sh2)


FFN_CONV_ROWS = 64
FFN_CONV_W = 128
FFN_DOWN_ROWS = 512


def _matmul_kernel(x_ref, w_ref, o_ref, w16):
    @pl.when(pl.program_id(1) == 0)
    def _():
        w16[...] = w_ref[...].astype(BF16)

    o_ref[...] = _dot(x_ref[...], w16[...]).astype(o_ref.dtype)


def _matmul_call(x, w, tm, tn, out_dtype, name):
    m, k = x.shape
    n = w.shape[1]
    return pl.pallas_call(
        _matmul_kernel,
        grid=(n // tn, m // tm),
        in_specs=[pl.BlockSpec((tm, k), lambda j, i: (i, 0)), pl.BlockSpec((k, tn), lambda j, i: (0, j))],
        out_specs=pl.BlockSpec((tm, tn), lambda j, i: (i, j)),
        out_shape=jax.ShapeDtypeStruct((m, n), out_dtype),
        scratch_shapes=[pltpu.VMEM((k, tn), BF16)],
        compiler_params=_cparams(("parallel", "arbitrary")),
        name=name,
    )(x, w)


def _ffn_tail_kernel(ua_m, ua_p, ua_n, ub_m, ub_p, ub_n, cwa, cwb, wd, x1, gt2, g_post2, o_ref, abuf, bbuf, acc):
    i = pl.program_id(0)
    c = pl.program_id(1)
    tm = ua_m.shape[0]
    fc = ua_m.shape[1]
    first = i == 0
    last = i == pl.num_programs(0) - 1

    @pl.when(c == 0)
    def _():
        acc[...] = jnp.zeros_like(acc)

    def fill(buf, main, prev, nxt):
        zero = jnp.zeros(prev.shape, prev.dtype)
        buf[0:GRID_W, :] = jnp.where(first, zero, prev[...])
        buf[GRID_W:GRID_W + tm, :] = main[...]
        buf[GRID_W + tm:GRID_W + tm + GRID_W, :] = jnp.where(last, zero, nxt[...])

    fill(abuf, ua_m, ua_p, ua_n)
    fill(bbuf, ub_m, ub_p, ub_n)

    nr = FFN_CONV_ROWS
    rt_ = lax.broadcasted_iota(jnp.int32, (nr, 3 * nr), 0)
    ct_ = lax.broadcasted_iota(jnp.int32, (nr, 3 * nr), 1)
    gcol = rt_ & (GRID_W - 1)
    hit = ((ct_ == rt_ - 1) & (gcol != 0)) | (ct_ == rt_ + nr) | ((ct_ == rt_ + 2 * nr + 1) & (gcol != GRID_W - 1))
    shift_add = jnp.where(hit, 1.0, 0.0).astype(BF16)

    def conv_rows(u_ref, cw_ref, r0):
        cw = cw_ref[...].astype(BF16)
        taps = [u_ref[r0 + dr * GRID_W:r0 + dr * GRID_W + nr, :] for dr in range(3)]
        col_sums = [cw[dc:dc + 1, :] * taps[0] + cw[3 + dc:4 + dc, :] * taps[1] + cw[6 + dc:7 + dc, :] * taps[2]
                    for dc in range(3)]
        return _dot(shift_add, jnp.concatenate(col_sums, axis=0))

    for m0 in range(0, tm, FFN_DOWN_ROWS):
        row_blocks = []
        for r0 in range(m0, m0 + FFN_DOWN_ROWS, nr):
            a = conv_rows(abuf, cwa, r0)
            b = conv_rows(bbuf, cwb, r0)
            row_blocks.append((a * _sigmoid(a) * b).astype(BF16))
        acc[m0:m0 + FFN_DOWN_ROWS, :] += _dot(jnp.concatenate(row_blocks, axis=0), wd[...])

    @pl.when(c == pl.num_programs(1) - 1)
    def _():
        f = acc[...]
        ms = jnp.mean(f * f, axis=-1, keepdims=True)
        o_ref[...] = x1[...] + gt2[...] * (f * lax.rsqrt(ms + RMS_EPS) * g_post2[...])


def _ffn_tail_call(up, cw, w_down, x1, gt2, g_post2, tm, fc):
    t, d = x1.shape
    dff = w_down.shape[0]
    nc = dff // fc
    hb = tm // GRID_W
    nhb = t // GRID_W

    def trio(off):
        return [pl.BlockSpec((tm, fc), lambda i, c: (i, off + c)),
                pl.BlockSpec((GRID_W, fc), lambda i, c: (jnp.maximum(i * hb - 1, 0), off + c)),
                pl.BlockSpec((GRID_W, fc), lambda i, c: (jnp.minimum((i + 1) * hb, nhb - 1), off + c))]

    tok = pl.BlockSpec((tm, d), lambda i, c: (i, 0))
    vec = pl.BlockSpec((1, d), lambda i, c: (0, 0))
    rows_all = tm + 2 * GRID_W
    return pl.pallas_call(
        _ffn_tail_kernel,
        grid=(t // tm, nc),
        in_specs=trio(0) + trio(nc) + [pl.BlockSpec((9, fc), lambda i, c: (0, c)),
                                       pl.BlockSpec((9, fc), lambda i, c: (0, nc + c)),
                                       pl.BlockSpec((fc, d), lambda i, c: (c, 0)),
                                       tok, vec, vec],
        out_specs=tok,
        out_shape=jax.ShapeDtypeStruct((t, d), F32),
        scratch_shapes=[pltpu.VMEM((rows_all, fc), BF16), pltpu.VMEM((rows_all, fc), BF16), pltpu.VMEM((tm, d), F32)],
        compiler_params=_cparams(("parallel", "arbitrary")),
        name="ffn_tail",
    )(*([up] * 6), cw, cw, w_down, x1, gt2, g_post2)


def _pad_lora(a, axis):
    pad = [(0, 0)] * a.ndim
    pad[axis] = (0, LORA_PAD - a.shape[axis])
    return jnp.pad(a, pad)


def kernel(x, c, ctx, c_ctx, w_mod, b_mod, g_pre1, g_post1, g_pre2, g_post2, w_in, conv_rwkv, w0, w_decay_up, a0, w_iclr_up, w_gate_up, k_k, k_a, r_k, lnx_g, lnx_b, ln_v_g, ln_v_b, w_spatial, b_spatial, w_proj_a, w_proj_b, w_out, w_up, conv_ffn, w_down):
    assert x.shape[0] == 1 and w_mod.shape[0] == 1
    xt, ct = x[0], ctx[0]
    t, d = xt.shape
    tc = ct.shape[0]
    row = lambda a: a.reshape(1, -1)

    off_dec = 3 * d
    off_iclr = off_dec + 2 * DECAY_LORA
    off_gate = off_iclr + 2 * ICLR_LORA
    rwkv_cols = off_gate + GATE_LORA

    def lora_layout(a):
        parts = [_pad_lora(a[:, off_dec + k * DECAY_LORA:off_dec + (k + 1) * DECAY_LORA], 1) for k in range(2)]
        parts += [_pad_lora(a[:, off_iclr + k * ICLR_LORA:off_iclr + (k + 1) * ICLR_LORA], 1) for k in range(2)]
        parts += [a[:, off_gate:rwkv_cols], jnp.zeros((a.shape[0], LORA_COLS - 4 * LORA_PAD - GATE_LORA), a.dtype)]
        return jnp.concatenate(parts, axis=1)

    wi = w_in[0]
    w_lora = lora_layout(wi).astype(BF16)
    cr = conv_rwkv[0]
    cw_r, cw_k, cw_v, cw_l = cr[:, :d], cr[:, d:2 * d], cr[:, 2 * d:3 * d], lora_layout(cr)
    wdec = _pad_lora(w_decay_up[0], 1).astype(BF16)
    wiclr = _pad_lora(w_iclr_up[0], 1).astype(BF16)
    wgate = w_gate_up[0].astype(BF16)
    ones_bd = jnp.kron(jnp.eye(SCAN_G, dtype=F32), jnp.ones((HEAD, HEAD), F32)).astype(BF16)
    gch = ln_v_g.shape[1] // GMLP_GROUPS
    bsp = jnp.repeat(b_spatial[0].T, gch, axis=1)
    cw_ffn = conv_ffn[0].reshape(9, -1)

    mod = _mod_call(jnp.stack([c[0], c_ctx], axis=1), w_mod[0], row(b_mod[0]))
    sh1, sc1, gt1, sh2, sc2, gt2 = [mod[0:1, k * d:(k + 1) * d] for k in range(6)]
    csh1, csc1 = mod[1:2, 0:d], mod[1:2, d:2 * d]

    tile = TILES
    tn = tile["inproj_cols"]
    n_rkv = off_dec // tn
    n_rest = (wi.shape[1] - rwkv_cols) // tn
    n_blocks = n_rkv + n_rest + 1
    hx = _prenorm_call(xt, row(g_pre1[0]), sc1, sh1, min(tile["prenorm_rows"], t))
    hc = _prenorm_call(ct, row(g_pre1[0]), csc1, csh1, min(tile["prenorm_rows"], tc))
    p = _inproj_call(hx, wi, w_lora, n_rkv, rwkv_cols - off_dec, n_rkv + n_rest, min(tile["inproj_rows"], t), tn)
    pc = _inproj_call(hc, wi, w_lora, n_rkv, 0, n_rkv, min(tile["inproj_ctx_rows"], tc), tn)

    targs = (cw_r, cw_k, cw_v, cw_l, w0[0], wdec, a0[0], wiclr, wgate, row(k_k[0]), row(k_a[0]), row(r_k[0]), ones_bd)
    r_c, v_c, kk_c, lw_c, kd_c, b_c, _, _ = _terms_call(pc, n_rkv, *targs, tm=min(tile["terms_rows"], tc))
    r_x, v_x, kk_x, lw_x, kd_x, b_x, g_x, bonus_x = _terms_call(p, n_blocks - 1, *targs,
                                                                tm=min(tile["terms_rows"], t))
    zero_state = jnp.zeros((d // SCAN_W, SCAN_W, SCAN_W), F32)
    ys = []
    for dirn in range(2):
        _, s_ctx = _scan_call(r_c, v_c, kk_c, lw_c, kd_c, b_c, zero_state, dirn, min(tile["scan_ctx_rows"], tc))
        y_d, _ = _scan_call(r_x, v_x, kk_x, lw_x, kd_x, b_x, s_ctx, dirn, min(tile["scan_rows"], t))
        ys.append(y_d)

    m = _merge_call(ys[0], ys[1], bonus_x, g_x, p, row(lnx_g[0]), row(lnx_b[0]), row(ln_v_g[0]), row(ln_v_b[0]),
                    w_spatial[0].astype(BF16), bsp, w_proj_a[0].astype(BF16), w_proj_b[0].astype(BF16), ones_bd,
                    tm=min(tile["merge_rows"], t))
    x1, h2 = _outproj_call(m, w_out[0], xt, gt1, row(g_post1[0]), row(g_pre2[0]), sc2, sh2,
                           tm=min(tile["outproj_rows"], t))

    up = _matmul_call(h2, w_up[0], min(tile["ffn_up_rows"], t), tile["ffn_up_cols"], BF16, "ffn_up")
    out = _ffn_tail_call(up, cw_ffn, w_down[0].astype(BF16), x1, gt2, row(g_post2[0]),
                         tm=min(tile["ffn_tail_rows"], t), fc=tile["ffn_tail_cols"])
    return out[None]
```
